```python
import math, functools
import jax, jax.numpy as jnp
from jax import lax
import numpy as np

D_MODEL = 1024
BATCH = 8
SEQ = 2048
DEPTH = 2
DEC_BATCH = 128
DEC_SEQ = 8
PAST_LEN = 2048
PAGE_SIZE = 128

HEAD_DIM = 64
A_HEADS = 4
A_WIDTH = A_HEADS * HEAD_DIM
A_CHUNK = 128
B_HEADS = 6
B_WIDTH = B_HEADS * HEAD_DIM
B_DECAY_RANK = 64
B_ICLR_RANK = 64
B_SHIFT_W = 3 * B_WIDTH + B_DECAY_RANK + B_ICLR_RANK
C_HEADS = 6
C_WIDTH = C_HEADS * HEAD_DIM
MOBA_BLOCK = 256
MOBA_TOPK = 3
MOBA_QBLOCK = 64
ROPE_THETA = 10000.0
MIX_WIDTH = A_WIDTH + B_WIDTH + C_WIDTH
A_COLS = 3 * A_WIDTH
B_COLS = B_SHIFT_W + B_WIDTH
C_COLS = 4 * C_WIDTH
IN_COLS = A_COLS + B_COLS + C_COLS
RMS_EPS = 1e-6
LN_EPS = 1e-5
GN_EPS = 64e-5

kernel_name = 'hymba_gmlp_rwkv7_moba_step'


def rms_norm(x, g):
    x32 = x.astype(jnp.float32)
    y = x32 * lax.rsqrt(jnp.mean(x32 * x32, axis=-1, keepdims=True) + RMS_EPS)
    return (y * g.astype(jnp.float32)).astype(x.dtype)


def rope(x, pos0):
    seq_len = x.shape[1]
    half = HEAD_DIM // 2
    pos = (pos0 + jnp.arange(seq_len)).astype(jnp.float32)
    inv = ROPE_THETA ** (-jnp.arange(half, dtype=jnp.float32) / half)
    ang = pos[:, None] * inv[None, :]
    cos = jnp.cos(ang)[None, :, None, :]
    sin = jnp.sin(ang)[None, :, None, :]
    x32 = x.astype(jnp.float32)
    x1, x2 = x32[..., :half], x32[..., half:]
    return jnp.concatenate([x1 * cos - x2 * sin, x2 * cos + x1 * sin], axis=-1).astype(x.dtype)


def gmlp_mixer(u, v, ln_g, ln_b, ws, bs):
    bsz, seq_len, _ = u.shape
    v32 = v.reshape(bsz, seq_len, A_HEADS, HEAD_DIM).astype(jnp.float32)
    mu = jnp.mean(v32, axis=-1, keepdims=True)
    var = jnp.mean(jnp.square(v32 - mu), axis=-1, keepdims=True)
    vn = ((v32 - mu) * lax.rsqrt(var + LN_EPS) * ln_g.astype(jnp.float32)
          + ln_b.astype(jnp.float32)).astype(v.dtype)
    n_chunks = -(-seq_len // A_CHUNK)
    pad = n_chunks * A_CHUNK - seq_len
    vc = jnp.pad(vn, ((0, 0), (0, pad), (0, 0), (0, 0))).reshape(
        bsz, n_chunks, A_CHUNK, A_HEADS, HEAD_DIM)
    causal = jnp.tril(jnp.ones((A_CHUNK, A_CHUNK), dtype=bool))
    wm = jnp.where(causal[None], ws, jnp.zeros_like(ws))
    mixed = jnp.einsum('hij,bcjhd->bcihd', wm, vc) + bs.T[None, None, :, :, None]
    mixed = mixed.reshape(bsz, n_chunks * A_CHUNK, A_WIDTH)[:, :seq_len]
    return u * mixed, vn.reshape(bsz, seq_len, A_WIDTH)


def rwkv7_mixer(sb, shift_prev, wkv_prev, mu, w0, w2, a0, a2, k_k, k_a, r_k, lnx_g, lnx_b):
    f32 = jnp.float32
    bsz, seq_len, _ = sb.shape
    prev = jnp.concatenate([shift_prev[:, None, :].astype(sb.dtype), sb[:, :-1]], axis=1)
    xs = sb + (prev - sb) * mu
    r, k, v, wd, ad = jnp.split(
        xs, [B_WIDTH, 2 * B_WIDTH, 3 * B_WIDTH, 3 * B_WIDTH + B_DECAY_RANK], axis=-1)
    w_log = -jax.nn.softplus(-(w0 + jnp.tanh(wd) @ w2).astype(f32)) - 0.5
    decay = jnp.exp(-jnp.exp(w_log))
    a = jax.nn.sigmoid((a0 + ad @ a2).astype(f32))

    def heads(t):
        return t.reshape(bsz, seq_len, B_HEADS, HEAD_DIM).astype(f32)

    r, k, v, decay, a = heads(r), heads(k), heads(v), heads(decay), heads(a)
    kk = k * k_k.reshape(B_HEADS, HEAD_DIM).astype(f32)
    kk = kk * lax.rsqrt(jnp.maximum(jnp.sum(kk * kk, axis=-1, keepdims=True), 1e-24))
    k = k * (1.0 + (a - 1.0) * k_a.reshape(B_HEADS, HEAD_DIM).astype(f32))

    def step(state, inp):
        r_t, w_t, k_t, v_t, kk_t, a_t = inp
        sa = jnp.einsum('bhvk,bhk->bhv', state, -kk_t)
        state = (state * w_t[:, :, None, :] + sa[..., None] * (kk_t * a_t)[:, :, None, :]
                 + v_t[..., None] * k_t[:, :, None, :])
        return state, jnp.einsum('bhvk,bhk->bhv', state, r_t)

    seq_major = tuple(jnp.moveaxis(t, 1, 0) for t in (r, decay, k, v, kk, a))
    wkv_new, out = lax.scan(step, wkv_prev.astype(f32), seq_major)
    out = jnp.moveaxis(out, 0, 1)
    m = jnp.mean(out, axis=-1, keepdims=True)
    var = jnp.mean(jnp.square(out - m), axis=-1, keepdims=True)
    out = (out - m) * lax.rsqrt(var + GN_EPS) * lnx_g.astype(f32) + lnx_b.astype(f32)
    out = out + jnp.sum(r * k * r_k.astype(f32), axis=-1, keepdims=True) * v
    return (out.reshape(bsz, seq_len, B_WIDTH).astype(sb.dtype),
            wkv_new.astype(sb.dtype), sb[:, -1])


def moba_attend(q, k, v, q_start):
    seq_len = q.shape[0]
    total = k.shape[0]
    n_blk = -(-total // MOBA_BLOCK)
    padk = n_blk * MOBA_BLOCK - total
    kb = jnp.pad(k, ((0, padk), (0, 0), (0, 0))).reshape(
        n_blk, MOBA_BLOCK, C_HEADS, HEAD_DIM).transpose(2, 0, 1, 3)
    vb = jnp.pad(v, ((0, padk), (0, 0), (0, 0))).reshape(
        n_blk, MOBA_BLOCK, C_HEADS, HEAD_DIM).transpose(2, 0, 1, 3)
    kmean = jnp.mean(kb.astype(jnp.float32), axis=2)
    n_sel = min(MOBA_TOPK, n_blk - 1)
    qb = min(MOBA_QBLOCK, seq_len)
    n_qb = -(-seq_len // qb)
    padq = n_qb * qb - seq_len
    qp = jnp.pad(q, ((0, padq), (0, 0), (0, 0))).reshape(n_qb, qb, C_HEADS, HEAD_DIM)
    qpos = (q_start + jnp.arange(n_qb * qb)).reshape(n_qb, qb)
    scale = HEAD_DIM ** -0.5
    hidx = jnp.arange(C_HEADS)[None, :, None]

    def one_block(args):
        qblk, pos = args
        own = jnp.minimum(pos // MOBA_BLOCK, n_blk - 1)
        own_sel = jnp.broadcast_to(own[:, None, None], (qb, C_HEADS, 1))
        if n_sel > 0:
            gate = jnp.einsum('qhd,hnd->qhn', qblk.astype(jnp.float32), kmean)
            past = jnp.arange(n_blk)[None, None, :] < own[:, None, None]
            gate = jnp.where(past, gate, -jnp.inf)
            _, top = lax.top_k(gate, n_sel)
            sel = jnp.concatenate([top, own_sel], axis=-1)
            sel_ok = jnp.concatenate(
                [top < own[:, None, None], jnp.ones((qb, C_HEADS, 1), dtype=bool)], axis=-1)
        else:
            sel = own_sel
            sel_ok = jnp.ones((qb, C_HEADS, 1), dtype=bool)
        kg = kb[hidx, sel]
        vg = vb[hidx, sel]
        logits = jnp.einsum('qhd,qhsjd->qhsj', qblk, kg).astype(jnp.float32) * scale
        kpos = sel[..., None] * MOBA_BLOCK + jnp.arange(MOBA_BLOCK)
        ok = sel_ok[..., None] & (kpos <= pos[:, None, None, None])
        logits = jnp.where(ok, logits, -jnp.inf)
        p = jax.nn.softmax(logits.reshape(qb, C_HEADS, -1), axis=-1).reshape(logits.shape)
        return jnp.einsum('qhsj,qhsjd->qhd', p.astype(vg.dtype), vg)

    out = lax.map(one_block, (qp, qpos))
    return out.reshape(n_qb * qb, C_HEADS, HEAD_DIM)[:seq_len]


def prompt_attend(q, k, v):
    return lax.map(lambda a: moba_attend(a[0], a[1], a[2], 0), (q, k, v))


def sample_attend(q, k, v, ck, cv, page_table):
    def one_seq(args):
        qs, ks, vs, pages = args
        kp = ck[pages].reshape(-1, C_HEADS, HEAD_DIM)
        vp = cv[pages].reshape(-1, C_HEADS, HEAD_DIM)
        k_all = jnp.concatenate([kp.astype(ks.dtype), ks], axis=0)
        v_all = jnp.concatenate([vp.astype(vs.dtype), vs], axis=0)
        return moba_attend(qs, k_all, v_all, kp.shape[0])
    return lax.map(one_seq, (q, k, v, page_table))


def decoder_layer(x, shift_prev, wkv_prev, attend, pos0, norm_g, w_in, w_out,
                  a_ln_g, a_ln_b, a_ws, a_bs, b_mu, b_w0, b_w2, b_a0, b_a2, b_kk, b_ka,
                  b_rk, b_lnx_g, b_lnx_b, c_qn_g, c_kn_g):
    bsz, seq_len, _ = x.shape
    h = rms_norm(x, norm_g)
    p = h @ w_in
    pa, pb, pc = jnp.split(p, [A_COLS, A_COLS + B_COLS], axis=-1)
    ua, va, za = jnp.split(pa, 3, axis=-1)
    ya, va_rows = gmlp_mixer(ua, va, a_ln_g, a_ln_b, a_ws, a_bs)
    sb, zb = pb[..., :B_SHIFT_W], pb[..., B_SHIFT_W:]
    yb, wkv_new, shift_new = rwkv7_mixer(sb, shift_prev, wkv_prev, b_mu, b_w0, b_w2, b_a0,
                                         b_a2, b_kk, b_ka, b_rk, b_lnx_g, b_lnx_b)
    qc, kc, vc, zc = jnp.split(pc, 4, axis=-1)
    q = rope(rms_norm(qc.reshape(bsz, seq_len, C_HEADS, HEAD_DIM), c_qn_g), pos0)
    k = rope(rms_norm(kc.reshape(bsz, seq_len, C_HEADS, HEAD_DIM), c_kn_g), pos0)
    v = vc.reshape(bsz, seq_len, C_HEADS, HEAD_DIM)
    yc = attend(q, k, v).reshape(bsz, seq_len, C_WIDTH)
    mixed = jnp.concatenate(
        [ya * jax.nn.silu(za), yb * jax.nn.silu(zb), yc * jax.nn.silu(zc)], axis=-1)
    return x + mixed @ w_out, k, v, wkv_new, shift_new, va_rows


def setup_inputs(seed: int = 0) -> dict:
    key = jax.random.key(seed)
    ks = jax.random.split(key, 32)
    f32 = jnp.float32
    n_pages = PAST_LEN // PAGE_SIZE
    n_used = DEC_BATCH * n_pages
    n_phys = n_used + (n_used + 3) // 4
    nrm = lambda i, shape, s: jax.random.normal(ks[i], shape, f32) * s
    page_table = jax.random.permutation(ks[7], n_phys)[:n_used].reshape(
        DEC_BATCH, n_pages).astype(jnp.int32)
    return {
        'x_prompt': nrm(0, (BATCH, SEQ, D_MODEL), 1.0),
        'x_sample': nrm(1, (DEC_BATCH, DEC_SEQ, D_MODEL), 1.0),
        'cache_k': nrm(2, (DEPTH, n_phys, PAGE_SIZE, C_HEADS, HEAD_DIM), 1.0),
        'cache_v': nrm(3, (DEPTH, n_phys, PAGE_SIZE, C_HEADS, HEAD_DIM), 1.0),
        'state_wkv': nrm(4, (DEPTH, DEC_BATCH, B_HEADS, HEAD_DIM, HEAD_DIM), 0.5),
        'state_shift': nrm(5, (DEPTH, DEC_BATCH, B_SHIFT_W), 1.0),
        'page_table': page_table,
        'norm_g': 1.0 + nrm(8, (DEPTH, D_MODEL), 0.02),
        'w_in': nrm(9, (DEPTH, D_MODEL, IN_COLS), D_MODEL ** -0.5),
        'w_out': nrm(10, (DEPTH, MIX_WIDTH, D_MODEL), MIX_WIDTH ** -0.5),
        'a_ln_g': 1.0 + nrm(11, (DEPTH, A_HEADS, HEAD_DIM), 0.02),
        'a_ln_b': nrm(12, (DEPTH, A_HEADS, HEAD_DIM), 0.02),
        'a_ws': nrm(13, (DEPTH, A_HEADS, A_CHUNK, A_CHUNK), A_CHUNK ** -0.5),
        'a_bs': 1.0 + nrm(14, (DEPTH, A_HEADS, A_CHUNK), 0.02),
        'b_mu': jax.random.uniform(ks[15], (DEPTH, B_SHIFT_W), f32),
        'b_w0': jax.random.uniform(ks[16], (DEPTH, B_WIDTH), f32, -6.0, 1.0),
        'b_w2': nrm(17, (DEPTH, B_DECAY_RANK, B_WIDTH), 0.1),
        'b_a0': nrm(18, (DEPTH, B_WIDTH), 0.1),
        'b_a2': nrm(19, (DEPTH, B_ICLR_RANK, B_WIDTH), 0.5 * B_ICLR_RANK ** -0.5),
        'b_kk': 0.85 + nrm(20, (DEPTH, B_WIDTH), 0.02),
        'b_ka': 1.0 + nrm(21, (DEPTH, B_WIDTH), 0.02),
        'b_rk': nrm(22, (DEPTH, B_HEADS, HEAD_DIM), 0.1),
        'b_lnx_g': 1.0 + nrm(23, (DEPTH, B_HEADS, HEAD_DIM), 0.02),
        'b_lnx_b': nrm(24, (DEPTH, B_HEADS, HEAD_DIM), 0.02),
        'c_qn_g': 1.0 + nrm(25, (DEPTH, HEAD_DIM), 0.02),
        'c_kn_g': 1.0 + nrm(26, (DEPTH, HEAD_DIM), 0.02),
    }


def reference(x_prompt, x_sample, cache_k, cache_v, state_wkv, state_shift, page_table,
              norm_g, w_in, w_out, a_ln_g, a_ln_b, a_ws, a_bs, b_mu, b_w0, b_w2, b_a0,
              b_a2, b_kk, b_ka, b_rk, b_lnx_g, b_lnx_b, c_qn_g, c_kn_g):
    hp, hs = x_prompt, x_sample
    p_bsz = x_prompt.shape[0]
    zero_shift = jnp.zeros((p_bsz, B_SHIFT_W), x_prompt.dtype)
    zero_wkv = jnp.zeros((p_bsz, B_HEADS, HEAD_DIM, HEAD_DIM), jnp.float32)
    kp_l, vp_l, wkvp_l, shp_l = [], [], [], []
    ks_l, vs_l, wkvs_l, shs_l, gvs_l = [], [], [], [], []
    for l in range(DEPTH):
        hp, kp, vp, wkvp, shp, _ = decoder_layer(
            hp, zero_shift, zero_wkv, prompt_attend, 0, norm_g[l], w_in[l], w_out[l],
            a_ln_g[l], a_ln_b[l], a_ws[l], a_bs[l], b_mu[l], b_w0[l], b_w2[l], b_a0[l],
            b_a2[l], b_kk[l], b_ka[l], b_rk[l], b_lnx_g[l], b_lnx_b[l], c_qn_g[l], c_kn_g[l])
        attend_s = functools.partial(sample_attend, ck=cache_k[l], cv=cache_v[l],
                                     page_table=page_table)
        hs, ksm, vsm, wkvs, shs, gvs = decoder_layer(
            hs, state_shift[l], state_wkv[l], attend_s, PAST_LEN, norm_g[l], w_in[l], w_out[l],
            a_ln_g[l], a_ln_b[l], a_ws[l], a_bs[l], b_mu[l], b_w0[l], b_w2[l], b_a0[l],
            b_a2[l], b_kk[l], b_ka[l], b_rk[l], b_lnx_g[l], b_lnx_b[l], c_qn_g[l], c_kn_g[l])
        kp_l.append(kp); vp_l.append(vp); wkvp_l.append(wkvp); shp_l.append(shp)
        ks_l.append(ksm); vs_l.append(vsm); wkvs_l.append(wkvs); shs_l.append(shs)
        gvs_l.append(gvs)
    k_prompt = jnp.stack(kp_l)
    v_prompt = jnp.stack(vp_l)
    k_sample = jnp.stack(ks_l)
    v_sample = jnp.stack(vs_l)
    wkv_prompt = jnp.stack(wkvp_l)
    wkv_sample = jnp.stack(wkvs_l)
    shift_prompt = jnp.stack(shp_l)
    shift_sample = jnp.stack(shs_l)
    gmlp_v_sample = jnp.stack(gvs_l)
    return (hp, hs, k_prompt, v_prompt, k_sample, v_sample, wkv_prompt, wkv_sample,
            shift_prompt, shift_sample, gmlp_v_sample)
```

```python
import functools
import math

import jax
import jax.numpy as jnp
from jax import lax
from jax.experimental import pallas as pl
from jax.experimental.pallas import tpu as pltpu

F32 = jnp.float32
BF16 = jnp.bfloat16
HI = lax.Precision.HIGHEST

HEAD_DIM = 64
A_HEADS = 4
A_WIDTH = A_HEADS * HEAD_DIM
A_CHUNK = 128
B_HEADS = 6
B_WIDTH = B_HEADS * HEAD_DIM
B_LORA = 64
B_SHIFT_W = 3 * B_WIDTH + 2 * B_LORA
C_HEADS = 6
C_WIDTH = C_HEADS * HEAD_DIM
MOBA_BLOCK = 256
MOBA_TOPK = 3
ROPE_THETA = 10000.0
A_COLS = 3 * A_WIDTH
B_COLS = B_SHIFT_W + B_WIDTH
C_COLS = 4 * C_WIDTH
RMS_EPS = 1e-6
LN_EPS = 1e-5
GN_EPS = 64e-5
NEG_BIG = -1e30

VMEM_LIMIT_BYTES = 56 * 1024 * 1024
RWKV_CHUNK = 64


def _cparams(sem):
    return pltpu.CompilerParams(dimension_semantics=sem,
                                vmem_limit_bytes=VMEM_LIMIT_BYTES)


def _dot(a, b, precision=None):
    return jnp.dot(a, b, preferred_element_type=F32, precision=precision)


def _dot_nt(a, b, precision=None):
    return lax.dot_general(a, b, (((1,), (1,)), ((), ())),
                           preferred_element_type=F32, precision=precision)


def _dot_tn(a, b, precision=None):
    return lax.dot_general(a, b, (((0,), (0,)), ((), ())),
                           preferred_element_type=F32, precision=precision)


def _silu(z):
    return z * jax.nn.sigmoid(z)


def _in_proj_kernel(x_ref, g_ref, w_ref, pa_ref, pb_ref, pc_ref):
    x = x_ref[...]
    ms = jnp.mean(x * x, axis=-1, keepdims=True)
    h = (x * lax.rsqrt(ms + RMS_EPS) * g_ref[...]).astype(BF16)
    p = _dot(h, w_ref[...])
    pa_ref[...] = p[:, :A_COLS]
    pb_ref[...] = p[:, A_COLS:A_COLS + B_COLS]
    pc_ref[...] = p[:, A_COLS + B_COLS:]


def _in_proj(x2d, g, w_bf16, tm):
    t, d = x2d.shape
    n = w_bf16.shape[1]
    assert t % tm == 0
    return pl.pallas_call(
        _in_proj_kernel,
        grid=(t // tm,),
        in_specs=[pl.BlockSpec((tm, d), lambda i: (i, 0)),
                  pl.BlockSpec((1, d), lambda i: (0, 0)),
                  pl.BlockSpec((d, n), lambda i: (0, 0))],
        out_specs=[pl.BlockSpec((tm, A_COLS), lambda i: (i, 0)),
                   pl.BlockSpec((tm, B_COLS), lambda i: (i, 0)),
                   pl.BlockSpec((tm, C_COLS), lambda i: (i, 0))],
        out_shape=[jax.ShapeDtypeStruct((t, A_COLS), F32),
                   jax.ShapeDtypeStruct((t, B_COLS), F32),
                   jax.ShapeDtypeStruct((t, C_COLS), F32)],
        compiler_params=_cparams(("parallel",)),
        name="in_proj",
    )(x2d, g, w_bf16)


def _gmlp_kernel(pa_ref, ws_ref, bs_ref, lng_ref, lnb_ref, gsum_ref, ya_ref, vn_ref,
                 *, n_sub):
    rows = pa_ref.shape[0]
    sub = rows // n_sub
    u = pa_ref[:, 0:A_WIDTH]
    v = pa_ref[:, A_WIDTH:2 * A_WIDTH]
    z = pa_ref[:, 2 * A_WIDTH:3 * A_WIDTH]
    gsum = gsum_ref[...]
    mu = _dot(v, gsum, HI) * (1.0 / HEAD_DIM)
    d = v - mu
    var = _dot(d * d, gsum, HI) * (1.0 / HEAD_DIM)
    vn = d * lax.rsqrt(var + LN_EPS) * lng_ref[...] + lnb_ref[...]
    vn_ref[...] = vn
    ri = lax.broadcasted_iota(jnp.int32, (sub, sub), 0)
    ci = lax.broadcasted_iota(jnp.int32, (sub, sub), 1)
    causal = ri >= ci
    vn_b = vn.astype(BF16)
    for h in range(A_HEADS):
        wm = jnp.where(causal, ws_ref[h], 0.0).astype(BF16)
        sl = slice(h * HEAD_DIM, (h + 1) * HEAD_DIM)
        for c in range(n_sub):
            rs = slice(c * sub, (c + 1) * sub)
            mixed = _dot(wm, vn_b[rs, sl]) + bs_ref[:, sl]
            ya_ref[rs, sl] = (u[rs, sl] * mixed * _silu(z[rs, sl])).astype(ya_ref.dtype)


def _gmlp(pa, ws_eff, bs_eff, lng, lnb, gsum, rows, n_sub):
    t = pa.shape[0]
    sub = rows // n_sub
    assert t % rows == 0 and ws_eff.shape == (A_HEADS, sub, sub)
    return pl.pallas_call(
        functools.partial(_gmlp_kernel, n_sub=n_sub),
        grid=(t // rows,),
        in_specs=[pl.BlockSpec((rows, A_COLS), lambda i: (i, 0)),
                  pl.BlockSpec((A_HEADS, sub, sub), lambda i: (0, 0, 0)),
                  pl.BlockSpec((sub, A_WIDTH), lambda i: (0, 0)),
                  pl.BlockSpec((1, A_WIDTH), lambda i: (0, 0)),
                  pl.BlockSpec((1, A_WIDTH), lambda i: (0, 0)),
                  pl.BlockSpec((A_WIDTH, A_WIDTH), lambda i: (0, 0))],
        out_specs=[pl.BlockSpec((rows, A_WIDTH), lambda i: (i, 0)),
                   pl.BlockSpec((rows, A_WIDTH), lambda i: (i, 0))],
        out_shape=[jax.ShapeDtypeStruct((t, A_WIDTH), BF16),
                   jax.ShapeDtypeStruct((t, A_WIDTH), F32)],
        compiler_params=_cparams(("parallel",)),
        name="gmlp",
    )(pa, ws_eff, bs_eff, lng, lnb, gsum)


def _rwkv_kernel(pb_ref, shift0_ref, wkv0_ref, mu_ref, w0_ref, w2_ref, a0_ref, a2_ref,
                 kk_ref, ka_ref, rk_ref, lng_ref, lnb_ref, gsum_ref,
                 yb_ref, wkv_ref, shift_ref, state_scr, prev_scr, o_scr):
    c = pl.program_id(1)
    n_c = pl.num_programs(1)
    rows = pb_ref.shape[0]

    @pl.when(c == 0)
    def _():
        state_scr[...] = wkv0_ref[...]
        prev_scr[...] = shift0_ref[...]

    sb = pb_ref[:, 0:B_SHIFT_W]
    zb = pb_ref[:, B_SHIFT_W:B_COLS]
    row_id = lax.broadcasted_iota(jnp.int32, (rows, 1), 0)
    prev = jnp.where(row_id == 0, prev_scr[...], pltpu.roll(sb, 1, axis=0))
    last_row = sb[rows - 1:rows, :]
    prev_scr[...] = last_row
    xs = sb + (prev - sb) * mu_ref[...]
    r = xs[:, 0:B_WIDTH]
    k = xs[:, B_WIDTH:2 * B_WIDTH]
    v = xs[:, 2 * B_WIDTH:3 * B_WIDTH]
    wd = xs[:, 3 * B_WIDTH:3 * B_WIDTH + B_LORA]
    ad = xs[:, 3 * B_WIDTH + B_LORA:B_SHIFT_W]

    gsum = gsum_ref[...]
    y = -(w0_ref[...] + _dot(jnp.tanh(wd).astype(BF16), w2_ref[...].astype(BF16)))
    softplus = jnp.maximum(y, 0.0) + jnp.log(1.0 + jnp.exp(-jnp.abs(y)))
    logdecay = -jnp.exp(-softplus - 0.5)
    a = jax.nn.sigmoid(a0_ref[...] + _dot(ad.astype(BF16), a2_ref[...].astype(BF16)))
    kk = k * kk_ref[...]
    kk = kk * lax.rsqrt(jnp.maximum(_dot(kk * kk, gsum, HI), 1e-24))
    k2 = k * (1.0 + (a - 1.0) * ka_ref[...])

    ri = lax.broadcasted_iota(jnp.int32, (rows, rows), 0)
    ci = lax.broadcasted_iota(jnp.int32, (rows, rows), 1)
    lower = ri >= ci
    strict = ri > ci
    cum = _dot(lower.astype(F32), logdecay, HI)
    cum_last = cum[rows - 1:rows, :]
    e_neg = jnp.exp(-cum)
    tail = jnp.exp(cum_last - cum)
    g_last = jnp.exp(cum_last)
    kka = kk * a
    at = -kk * jnp.exp(cum - logdecay)
    rt = r * jnp.exp(cum)
    bp = kka * e_neg
    kp = k2 * e_neg
    bh = kka * tail
    kh = k2 * tail
    eye = (ri == ci).astype(F32)
    n_sq = max(1, int(math.ceil(math.log2(rows)))) - 1

    for h in range(B_HEADS):
        sl = slice(h * HEAD_DIM, (h + 1) * HEAD_DIM)
        at_h, rt_h, bp_h, kp_h, v_h = at[:, sl], rt[:, sl], bp[:, sl], kp[:, sl], v[:, sl]
        s0 = state_scr[h]
        am = jnp.where(strict, _dot_nt(at_h, bp_h, HI), 0.0)
        bm = jnp.where(strict, _dot_nt(at_h, kp_h, HI), 0.0)
        pm = jnp.where(lower, _dot_nt(rt_h, bp_h, HI), 0.0)
        qm = jnp.where(lower, _dot_nt(rt_h, kp_h, HI), 0.0)
        tinv = eye + am
        apow = am
        for _ in range(n_sq):
            apow = _dot(apow, apow, HI)
            tinv = tinv + _dot(tinv, apow, HI)
        rhs = _dot_nt(at_h, s0, HI) + _dot(bm, v_h, HI)
        u = _dot(tinv, rhs, HI)
        o = _dot_nt(rt_h, s0, HI) + _dot(pm, u, HI) + _dot(qm, v_h, HI)
        state_scr[h] = (s0 * g_last[:, sl] + _dot_tn(u, bh[:, sl], HI)
                        + _dot_tn(v_h, kh[:, sl], HI))
        o_scr[:, sl] = o

    o = o_scr[...]
    m = _dot(o, gsum, HI) * (1.0 / HEAD_DIM)
    d = o - m
    var = _dot(d * d, gsum, HI) * (1.0 / HEAD_DIM)
    on = d * lax.rsqrt(var + GN_EPS) * lng_ref[...] + lnb_ref[...]
    bonus = _dot(r * k2 * rk_ref[...], gsum, HI) * v
    yb_ref[...] = ((on + bonus) * _silu(zb)).astype(yb_ref.dtype)

    @pl.when(c == n_c - 1)
    def _():
        wkv_ref[...] = state_scr[...]
        shift_ref[...] = last_row


def _rwkv(pb, shift0, wkv0, prm, gsum, n_seq, rows):
    t = pb.shape[0]
    n_c = t // (n_seq * rows)
    assert n_c * n_seq * rows == t
    vec = lambda w: pl.BlockSpec((1, w), lambda s, c: (0, 0))
    return pl.pallas_call(
        _rwkv_kernel,
        grid=(n_seq, n_c),
        in_specs=[pl.BlockSpec((rows, B_COLS), lambda s, c: (s * n_c + c, 0)),
                  pl.BlockSpec((None, 1, B_SHIFT_W), lambda s, c: (s, 0, 0)),
                  pl.BlockSpec((None, B_HEADS, HEAD_DIM, HEAD_DIM), lambda s, c: (s, 0, 0, 0)),
                  vec(B_SHIFT_W), vec(B_WIDTH),
                  pl.BlockSpec((B_LORA, B_WIDTH), lambda s, c: (0, 0)),
                  vec(B_WIDTH),
                  pl.BlockSpec((B_LORA, B_WIDTH), lambda s, c: (0, 0)),
                  vec(B_WIDTH), vec(B_WIDTH), vec(B_WIDTH), vec(B_WIDTH), vec(B_WIDTH),
                  pl.BlockSpec((B_WIDTH, B_WIDTH), lambda s, c: (0, 0))],
        out_specs=[pl.BlockSpec((rows, B_WIDTH), lambda s, c: (s * n_c + c, 0)),
                   pl.BlockSpec((None, B_HEADS, HEAD_DIM, HEAD_DIM), lambda s, c: (s, 0, 0, 0)),
                   pl.BlockSpec((None, 1, B_SHIFT_W), lambda s, c: (s, 0, 0))],
        out_shape=[jax.ShapeDtypeStruct((t, B_WIDTH), BF16),
                   jax.ShapeDtypeStruct((n_seq, B_HEADS, HEAD_DIM, HEAD_DIM), F32),
                   jax.ShapeDtypeStruct((n_seq, 1, B_SHIFT_W), F32)],
        scratch_shapes=[pltpu.VMEM((B_HEADS, HEAD_DIM, HEAD_DIM), F32),
                        pltpu.VMEM((1, B_SHIFT_W), F32),
                        pltpu.VMEM((rows, B_WIDTH), F32)],
        compiler_params=_cparams(("parallel", "arbitrary")),
        name="rwkv",
    )(pb, shift0, wkv0, prm["mu"], prm["w0"], prm["w2"], prm["a0"], prm["a2"],
      prm["kk"], prm["ka"], prm["rk"], prm["lnx_g"], prm["lnx_b"], gsum)


def _qk_norm_rope(x, g, cos, sin_signed, gsum):
    ms = _dot(x * x, gsum, HI) * (1.0 / HEAD_DIM)
    y = x * lax.rsqrt(ms + RMS_EPS) * g
    half = HEAD_DIM // 2
    lane = lax.broadcasted_iota(jnp.int32, (1, 128), 1)
    first = (lane % HEAD_DIM) < half
    outs = []
    for p in range(x.shape[1] // 128):
        yp = y[:, p * 128:(p + 1) * 128]
        partner = jnp.where(first, pltpu.roll(yp, 128 - half, axis=1),
                            pltpu.roll(yp, half, axis=1))
        outs.append(yp * cos + partner * sin_signed)
    return outs


def _prep_prompt_kernel(pc_ref, cos_ref, sin_ref, qg_ref, kg_ref, gsum_ref,
                        q_ref, kt_ref, vt_ref):
    gsum = gsum_ref[...]
    cos = cos_ref[...]
    sin = sin_ref[...]
    qs = _qk_norm_rope(pc_ref[:, 0:C_WIDTH], qg_ref[...], cos, sin, gsum)
    ks = _qk_norm_rope(pc_ref[:, C_WIDTH:2 * C_WIDTH], kg_ref[...], cos, sin, gsum)
    for p in range(C_HEADS // 2):
        vp = pc_ref[:, 2 * C_WIDTH + p * 128:2 * C_WIDTH + (p + 1) * 128]
        kpt = ks[p].T
        vpt = vp.T
        for j in range(2):
            h = 2 * p + j
            q_ref[h] = qs[p][:, j * HEAD_DIM:(j + 1) * HEAD_DIM]
            kt_ref[h] = kpt[j * HEAD_DIM:(j + 1) * HEAD_DIM, :]
            vt_ref[h] = vpt[j * HEAD_DIM:(j + 1) * HEAD_DIM, :]


def _prep_prompt(pc, cos, sin, qg, kg, gsum, n_seq, seq_len, tm):
    n_t = seq_len // tm
    assert n_t * tm == seq_len
    return pl.pallas_call(
        _prep_prompt_kernel,
        grid=(n_seq, n_t),
        in_specs=[pl.BlockSpec((tm, C_COLS), lambda s, i: (s * n_t + i, 0)),
                  pl.BlockSpec((tm, 128), lambda s, i: (i, 0)),
                  pl.BlockSpec((tm, 128), lambda s, i: (i, 0)),
                  pl.BlockSpec((1, C_WIDTH), lambda s, i: (0, 0)),
                  pl.BlockSpec((1, C_WIDTH), lambda s, i: (0, 0)),
                  pl.BlockSpec((C_WIDTH, C_WIDTH), lambda s, i: (0, 0))],
        out_specs=[pl.BlockSpec((None, C_HEADS, tm, HEAD_DIM), lambda s, i: (s, 0, i, 0)),
                   pl.BlockSpec((None, C_HEADS, HEAD_DIM, tm), lambda s, i: (s, 0, 0, i)),
                   pl.BlockSpec((None, C_HEADS, HEAD_DIM, tm), lambda s, i: (s, 0, 0, i))],
        out_shape=[jax.ShapeDtypeStruct((n_seq, C_HEADS, seq_len, HEAD_DIM), F32),
                   jax.ShapeDtypeStruct((n_seq, C_HEADS, HEAD_DIM, seq_len), F32),
                   jax.ShapeDtypeStruct((n_seq, C_HEADS, HEAD_DIM, seq_len), F32)],
        compiler_params=_cparams(("parallel", "parallel")),
        name="moba_prep_prompt",
    )(pc, cos, sin, qg, kg, gsum)


def _prep_sample_kernel(pc_ref, cos_ref, sin_ref, qg_ref, kg_ref, gsum_ref, q_ref, k_ref):
    gsum = gsum_ref[...]
    cos = cos_ref[...]
    sin = sin_ref[...]
    qs = _qk_norm_rope(pc_ref[:, 0:C_WIDTH], qg_ref[...], cos, sin, gsum)
    ks = _qk_norm_rope(pc_ref[:, C_WIDTH:2 * C_WIDTH], kg_ref[...], cos, sin, gsum)
    for p in range(C_HEADS // 2):
        q_ref[:, p * 128:(p + 1) * 128] = qs[p]
        k_ref[:, p * 128:(p + 1) * 128] = ks[p]


def _prep_sample(pc, cos, sin, qg, kg, gsum, tm):
    t = pc.shape[0]
    assert t % tm == 0
    return pl.pallas_call(
        _prep_sample_kernel,
        grid=(t // tm,),
        in_specs=[pl.BlockSpec((tm, C_COLS), lambda i: (i, 0)),
                  pl.BlockSpec((tm, 128), lambda i: (i, 0)),
                  pl.BlockSpec((tm, 128), lambda i: (i, 0)),
                  pl.BlockSpec((1, C_WIDTH), lambda i: (0, 0)),
                  pl.BlockSpec((1, C_WIDTH), lambda i: (0, 0)),
                  pl.BlockSpec((C_WIDTH, C_WIDTH), lambda i: (0, 0))],
        out_specs=[pl.BlockSpec((tm, C_WIDTH), lambda i: (i, 0)),
                   pl.BlockSpec((tm, C_WIDTH), lambda i: (i, 0))],
        out_shape=[jax.ShapeDtypeStruct((t, C_WIDTH), F32),
                   jax.ShapeDtypeStruct((t, C_WIDTH), F32)],
        compiler_params=_cparams(("parallel",)),
        name="moba_prep_sample",
    )(pc, cos, sin, qg, kg, gsum)


def _topk_past_mask(gates, n_valid):
    n = len(gates)
    valid = [jnp.where(j < n_valid, 1.0, 0.0).astype(F32) for j in range(n)]
    sel = []
    for j in range(n):
        rank = jnp.zeros(gates[j].shape, F32)
        for j2 in range(n):
            if j2 == j:
                continue
            beats = (gates[j2] > gates[j]) if j2 > j else (gates[j2] >= gates[j])
            rank = rank + jnp.where(beats, valid[j2], 0.0)
        sel.append(jnp.where(rank < MOBA_TOPK, valid[j], 0.0))
    return sel


def _moba_prompt_kernel(q_ref, kt_ref, vt_ref, z_ref, bmean_ref, y_ref, kmean_scr):
    i = pl.program_id(2)
    seq_len = kt_ref.shape[2]
    n_blk = seq_len // MOBA_BLOCK
    tq = q_ref.shape[1]
    scale = HEAD_DIM ** -0.5

    @pl.when(i == 0)
    def _():
        for j in range(2):
            kmean_scr[j] = _dot(kt_ref[j], bmean_ref[...], HI)

    qpos = lax.broadcasted_iota(jnp.int32, (tq, MOBA_BLOCK), 0) + i * MOBA_BLOCK
    col = lax.broadcasted_iota(jnp.int32, (tq, MOBA_BLOCK), 1)
    own_blk = jnp.full((tq, 1), i, jnp.int32)
    for j in range(2):
        q = q_ref[j]
        gate = _dot(q, kmean_scr[j], HI)
        gates = [gate[:, b:b + 1] for b in range(n_blk)]
        sel = _topk_past_mask(gates, own_blk)
        s = _dot(q.astype(BF16), kt_ref[j].astype(BF16)) * scale
        pieces = []
        for b in range(n_blk):
            sb = s[:, b * MOBA_BLOCK:(b + 1) * MOBA_BLOCK]
            kpos = col + b * MOBA_BLOCK
            own = (kpos <= qpos) & (kpos >= i * MOBA_BLOCK)
            ok = (jnp.broadcast_to(sel[b], (tq, MOBA_BLOCK)) > 0.5) | own
            pieces.append(jnp.where(ok, sb, NEG_BIG))
        m = pieces[0].max(axis=-1, keepdims=True)
        for b in range(1, n_blk):
            m = jnp.maximum(m, pieces[b].max(axis=-1, keepdims=True))
        acc = jnp.zeros((tq, HEAD_DIM), F32)
        den = jnp.zeros((tq, 1), F32)
        for b in range(n_blk):
            p = jnp.exp(pieces[b] - m)
            den = den + p.sum(axis=-1, keepdims=True)
            acc = acc + _dot_nt(p.astype(BF16),
                                vt_ref[j, :, b * MOBA_BLOCK:(b + 1) * MOBA_BLOCK].astype(BF16))
        out = acc / den
        zj = z_ref[:, j * HEAD_DIM:(j + 1) * HEAD_DIM]
        y_ref[:, j * HEAD_DIM:(j + 1) * HEAD_DIM] = (out * _silu(zj)).astype(y_ref.dtype)


def _moba_prompt(q, kt, vt, pc, bmean, n_seq, seq_len):
    n_q = seq_len // MOBA_BLOCK
    z_col0 = 3 * C_WIDTH // 128
    return pl.pallas_call(
        _moba_prompt_kernel,
        grid=(n_seq, C_HEADS // 2, n_q),
        in_specs=[pl.BlockSpec((None, 2, MOBA_BLOCK, HEAD_DIM), lambda s, p, i: (s, p, i, 0)),
                  pl.BlockSpec((None, 2, HEAD_DIM, seq_len), lambda s, p, i: (s, p, 0, 0)),
                  pl.BlockSpec((None, 2, HEAD_DIM, seq_len), lambda s, p, i: (s, p, 0, 0)),
                  pl.BlockSpec((MOBA_BLOCK, 128), lambda s, p, i: (s * n_q + i, z_col0 + p)),
                  pl.BlockSpec((seq_len, n_q), lambda s, p, i: (0, 0))],
        out_specs=pl.BlockSpec((MOBA_BLOCK, 128), lambda s, p, i: (s * n_q + i, p)),
        out_shape=jax.ShapeDtypeStruct((n_seq * seq_len, C_WIDTH), BF16),
        scratch_shapes=[pltpu.VMEM((2, HEAD_DIM, n_q), F32)],
        compiler_params=_cparams(("parallel", "parallel", "arbitrary")),
        name="moba_prompt",
    )(q, kt, vt, pc, bmean)


def _moba_sample_kernel(pt_ref, q_ref, kn_ref, pc_ref, k0_ref, k1_ref, v0_ref, v1_ref,
                        y_ref, m_scr, l_scr, o_scr, g_scr, *, n_past):
    del pt_ref
    j = pl.program_id(1)
    s_len = q_ref.shape[0]
    scale = HEAD_DIM ** -0.5
    page = k0_ref.shape[2]
    ones = jnp.ones((8, page), F32)

    for h in range(C_HEADS):
        sl = slice(h * HEAD_DIM, (h + 1) * HEAD_DIM)
        q = q_ref[:, sl]
        qb = q.astype(BF16)
        s0 = _dot(qb, k0_ref[h].astype(BF16)) * scale
        s1 = _dot(qb, k1_ref[h].astype(BF16)) * scale
        m = jnp.maximum(s0.max(axis=-1, keepdims=True), s1.max(axis=-1, keepdims=True))
        p0 = jnp.exp(s0 - m)
        p1 = jnp.exp(s1 - m)
        l = p0.sum(axis=-1, keepdims=True) + p1.sum(axis=-1, keepdims=True)
        o = (_dot_nt(p0.astype(BF16), v0_ref[h].astype(BF16))
             + _dot_nt(p1.astype(BF16), v1_ref[h].astype(BF16)))
        ksum = _dot_nt(ones, k0_ref[h], HI) + _dot_nt(ones, k1_ref[h], HI)
        kmean = ksum[0:1, :] * (1.0 / MOBA_BLOCK)
        gate = jnp.sum(q * kmean, axis=-1, keepdims=True)
        m_scr[j, h] = jnp.broadcast_to(m, (s_len, 128))
        l_scr[j, h] = jnp.broadcast_to(l, (s_len, 128))
        g_scr[j, h] = jnp.broadcast_to(gate, (s_len, 128))
        o_scr[j, h] = o

    @pl.when(j == n_past - 1)
    def _():
        row = lax.broadcasted_iota(jnp.int32, (s_len, s_len), 0)
        col = lax.broadcasted_iota(jnp.int32, (s_len, s_len), 1)
        for h in range(C_HEADS):
            sl = slice(h * HEAD_DIM, (h + 1) * HEAD_DIM)
            q = q_ref[:, sl]
            kn = kn_ref[:, sl]
            vn = pc_ref[:, 2 * C_WIDTH + h * HEAD_DIM:2 * C_WIDTH + (h + 1) * HEAD_DIM]
            z = pc_ref[:, 3 * C_WIDTH + h * HEAD_DIM:3 * C_WIDTH + (h + 1) * HEAD_DIM]
            so = _dot_nt(q.astype(BF16), kn.astype(BF16)) * scale
            so = jnp.where(col <= row, so, NEG_BIG)
            gates = [g_scr[b, h] for b in range(n_past)]
            sel = _topk_past_mask(gates, n_past)
            mo = so.max(axis=-1, keepdims=True)
            mm = jnp.broadcast_to(mo, (s_len, 128))
            for b in range(n_past):
                mm = jnp.where(sel[b] > 0.5, jnp.maximum(mm, m_scr[b, h]), mm)
            po = jnp.exp(so - mm[:, 0:1])
            den = jnp.broadcast_to(po.sum(axis=-1, keepdims=True), (s_len, 128))
            acc = _dot(po.astype(BF16), vn.astype(BF16))
            for b in range(n_past):
                w = jnp.where(sel[b] > 0.5, jnp.exp(m_scr[b, h] - mm), 0.0)
                den = den + w * l_scr[b, h]
                acc = acc + w[:, 0:HEAD_DIM] * o_scr[b, h]
            out = acc / den[:, 0:HEAD_DIM]
            y_ref[:, sl] = (out * _silu(z)).astype(y_ref.dtype)


def _moba_sample(page_table, qn, kn, pc, cache_kt, cache_vt, layer, n_seq, s_len):
    n_pages = page_table.shape[1]
    page = cache_kt.shape[-1]
    per_blk = MOBA_BLOCK // page
    n_past = n_pages // per_blk
    assert per_blk == 2 and n_past * per_blk == n_pages
    cache_spec = lambda o: pl.BlockSpec(
        (None, None, C_HEADS, HEAD_DIM, page),
        lambda s, j, pt: (layer, pt[s, per_blk * j + o], 0, 0, 0))
    grid_spec = pltpu.PrefetchScalarGridSpec(
        num_scalar_prefetch=1,
        grid=(n_seq, n_past),
        in_specs=[pl.BlockSpec((s_len, C_WIDTH), lambda s, j, pt: (s, 0)),
                  pl.BlockSpec((s_len, C_WIDTH), lambda s, j, pt: (s, 0)),
                  pl.BlockSpec((s_len, C_COLS), lambda s, j, pt: (s, 0)),
                  cache_spec(0), cache_spec(1), cache_spec(0), cache_spec(1)],
        out_specs=pl.BlockSpec((s_len, C_WIDTH), lambda s, j, pt: (s, 0)),
        scratch_shapes=[pltpu.VMEM((n_past, C_HEADS, s_len, 128), F32),
                        pltpu.VMEM((n_past, C_HEADS, s_len, 128), F32),
                        pltpu.VMEM((n_past, C_HEADS, s_len, HEAD_DIM), F32),
                        pltpu.VMEM((n_past, C_HEADS, s_len, 128), F32)])
    return pl.pallas_call(
        functools.partial(_moba_sample_kernel, n_past=n_past),
        grid_spec=grid_spec,
        out_shape=jax.ShapeDtypeStruct((n_seq * s_len, C_WIDTH), BF16),
        compiler_params=_cparams(("parallel", "arbitrary")),
        name="moba_sample",
    )(page_table, qn, kn, pc, cache_kt, cache_kt, cache_vt, cache_vt)


def _out_proj_kernel(x_ref, ya_ref, yb_ref, yc_ref, w_ref, o_ref):
    acc = _dot(ya_ref[...], w_ref[0:A_WIDTH, :])
    acc = acc + _dot(yb_ref[...], w_ref[A_WIDTH:A_WIDTH + B_WIDTH, :])
    acc = acc + _dot(yc_ref[...], w_ref[A_WIDTH + B_WIDTH:, :])
    o_ref[...] = x_ref[...] + acc


def _out_proj(x2d, ya, yb, yc, w_bf16, tm):
    t, d = x2d.shape
    assert t % tm == 0
    row = lambda w: pl.BlockSpec((tm, w), lambda i: (i, 0))
    return pl.pallas_call(
        _out_proj_kernel,
        grid=(t // tm,),
        in_specs=[row(d), row(A_WIDTH), row(B_WIDTH), row(C_WIDTH),
                  pl.BlockSpec(w_bf16.shape, lambda i: (0, 0))],
        out_specs=row(d),
        out_shape=jax.ShapeDtypeStruct((t, d), F32),
        compiler_params=_cparams(("parallel",)),
        name="out_proj",
    )(x2d, ya, yb, yc, w_bf16)


def _group_sum_matrix(width):
    g = jnp.arange(width) // HEAD_DIM
    return (g[:, None] == g[None, :]).astype(F32)


def _rope_tables(pos):
    half = HEAD_DIM // 2
    inv = ROPE_THETA ** (-jnp.arange(half, dtype=F32) / half)
    ang = pos.astype(F32)[:, None] * inv[None, :]
    cos = jnp.cos(ang)
    sin = jnp.sin(ang)
    return jnp.tile(cos, (1, 4)), jnp.tile(jnp.concatenate([-sin, sin], axis=1), (1, 2))


def _tile_heads(g, n_heads):
    return jnp.tile(g.reshape(1, HEAD_DIM), (1, n_heads))


def kernel(x_prompt, x_sample, cache_k, cache_v, state_wkv, state_shift, page_table, norm_g, w_in, w_out, a_ln_g, a_ln_b, a_ws, a_bs, b_mu, b_w0, b_w2, b_a0, b_a2, b_kk, b_ka, b_rk, b_lnx_g, b_lnx_b, c_qn_g, c_kn_g):
    depth = w_in.shape[0]
    n_p, seq_len, d_model = x_prompt.shape
    n_s, s_len, _ = x_sample.shape
    page = cache_k.shape[2]
    past_len = page_table.shape[1] * page
    assert seq_len % MOBA_BLOCK == 0 and seq_len % A_CHUNK == 0 and seq_len % RWKV_CHUNK == 0
    assert past_len % MOBA_BLOCK == 0 and s_len % 8 == 0 and s_len <= min(A_CHUNK, RWKV_CHUNK)
    n_blk = seq_len // MOBA_BLOCK

    gsum_a = _group_sum_matrix(A_WIDTH)
    gsum_b = _group_sum_matrix(B_WIDTH)
    cos_p, sin_p = _rope_tables(jnp.arange(seq_len))
    cos_s, sin_s = _rope_tables(past_len + jnp.arange(n_s * s_len) % s_len)
    bmean = ((jnp.arange(seq_len)[:, None] // MOBA_BLOCK == jnp.arange(n_blk)[None, :])
             .astype(F32) * (1.0 / MOBA_BLOCK))
    cache_kt = jnp.transpose(cache_k, (0, 1, 3, 4, 2))
    cache_vt = jnp.transpose(cache_v, (0, 1, 3, 4, 2))
    w_in_b = w_in.astype(BF16)
    w_out_b = w_out.astype(BF16)
    zero_shift = jnp.zeros((n_p, 1, B_SHIFT_W), F32)
    zero_wkv = jnp.zeros((n_p, B_HEADS, HEAD_DIM, HEAD_DIM), F32)
    seqs_per_tile = A_CHUNK // s_len
    eye_tile = jnp.eye(seqs_per_tile, dtype=F32)

    hp = x_prompt.reshape(n_p * seq_len, d_model)
    hs = x_sample.reshape(n_s * s_len, d_model)
    outs = {k: [] for k in ("kp", "vp", "ks", "vs", "wkvp", "wkvs", "shp", "shs", "gvs")}
    for l in range(depth):
        g = norm_g[l].reshape(1, d_model)
        lng_a = a_ln_g[l].reshape(1, A_WIDTH)
        lnb_a = a_ln_b[l].reshape(1, A_WIDTH)
        bs_full = jnp.repeat(a_bs[l].T, HEAD_DIM, axis=1)
        ws_s = jnp.stack([jnp.kron(eye_tile, a_ws[l, h, :s_len, :s_len]) for h in range(A_HEADS)])
        bs_s = jnp.tile(bs_full[:s_len], (seqs_per_tile, 1))
        prm = dict(mu=b_mu[l].reshape(1, -1), w0=b_w0[l].reshape(1, -1), w2=b_w2[l],
                   a0=b_a0[l].reshape(1, -1), a2=b_a2[l], kk=b_kk[l].reshape(1, -1),
                   ka=b_ka[l].reshape(1, -1), rk=b_rk[l].reshape(1, -1),
                   lnx_g=b_lnx_g[l].reshape(1, -1), lnx_b=b_lnx_b[l].reshape(1, -1))
        qg = _tile_heads(c_qn_g[l], C_HEADS)
        kg = _tile_heads(c_kn_g[l], C_HEADS)

        pa, pb, pc = _in_proj(hp, g, w_in_b[l], min(256, hp.shape[0]))
        ya, _ = _gmlp(pa, a_ws[l], bs_full, lng_a, lnb_a, gsum_a, 4 * A_CHUNK, 4)
        yb, wkvp, shp = _rwkv(pb, zero_shift, zero_wkv, prm, gsum_b, n_p, RWKV_CHUNK)
        q, kt, vt = _prep_prompt(pc, cos_p, sin_p, qg, kg, gsum_b, n_p, seq_len, MOBA_BLOCK)
        yc = _moba_prompt(q, kt, vt, pc, bmean, n_p, seq_len)
        hp = _out_proj(hp, ya, yb, yc, w_out_b[l], min(512, hp.shape[0]))
        outs["kp"].append(jnp.transpose(kt, (0, 3, 1, 2)))
        outs["vp"].append(jnp.transpose(vt, (0, 3, 1, 2)))
        outs["wkvp"].append(wkvp)
        outs["shp"].append(shp.reshape(n_p, B_SHIFT_W))

        pa, pb, pc = _in_proj(hs, g, w_in_b[l], min(256, hs.shape[0]))
        ya, gv = _gmlp(pa, ws_s, bs_s, lng_a, lnb_a, gsum_a, A_CHUNK, 1)
        yb, wkvs, shs = _rwkv(pb, state_shift[l].reshape(n_s, 1, B_SHIFT_W), state_wkv[l],
                              prm, gsum_b, n_s, s_len)
        qn, kn = _prep_sample(pc, cos_s, sin_s, qg, kg, gsum_b, min(256, hs.shape[0]))
        yc = _moba_sample(page_table, qn, kn, pc, cache_kt, cache_vt, l, n_s, s_len)
        hs = _out_proj(hs, ya, yb, yc, w_out_b[l], min(512, hs.shape[0]))
        outs["ks"].append(kn.reshape(n_s, s_len, C_HEADS, HEAD_DIM))
        outs["vs"].append(pc[:, 2 * C_WIDTH:3 * C_WIDTH].reshape(n_s, s_len, C_HEADS, HEAD_DIM))
        outs["wkvs"].append(wkvs)
        outs["shs"].append(shs.reshape(n_s, B_SHIFT_W))
        outs["gvs"].append(gv.reshape(n_s, s_len, A_WIDTH))

    st = {k: jnp.stack(v) for k, v in outs.items()}
    return (hp.reshape(n_p, seq_len, d_model), hs.reshape(n_s, s_len, d_model),
            st["kp"], st["vp"], st["ks"], st["vs"], st["wkvp"], st["wkvs"],
            st["shp"], st["shs"], st["gvs"])
```

```python
import functools
import math

import jax
import jax.numpy as jnp
from jax import lax
from jax.experimental import pallas as pl
from jax.experimental.pallas import tpu as pltpu

F32 = jnp.float32
BF16 = jnp.bfloat16
HI = lax.Precision.HIGHEST

HEAD_DIM = 64
A_HEADS = 4
A_WIDTH = A_HEADS * HEAD_DIM
A_CHUNK = 128
B_HEADS = 6
B_WIDTH = B_HEADS * HEAD_DIM
B_LORA = 64
B_SHIFT_W = 3 * B_WIDTH + 2 * B_LORA
C_HEADS = 6
C_WIDTH = C_HEADS * HEAD_DIM
MOBA_BLOCK = 256
MOBA_TOPK = 3
ROPE_THETA = 10000.0
A_COLS = 3 * A_WIDTH
B_COLS = B_SHIFT_W + B_WIDTH
C_COLS = 4 * C_WIDTH
RMS_EPS = 1e-6
LN_EPS = 1e-5
GN_EPS = 64e-5
NEG_BIG = -1e30

VMEM_LIMIT_BYTES = 56 * 1024 * 1024
RWKV_CHUNK = 64
RWKV_ROWS = 256


def _cparams(sem):
    return pltpu.CompilerParams(dimension_semantics=sem,
                                vmem_limit_bytes=VMEM_LIMIT_BYTES)


def _dot(a, b, precision=None):
    return jnp.dot(a, b, preferred_element_type=F32, precision=precision)


def _dot_nt(a, b, precision=None):
    return lax.dot_general(a, b, (((1,), (1,)), ((), ())),
                           preferred_element_type=F32, precision=precision)


def _dot_tn(a, b, precision=None):
    return lax.dot_general(a, b, (((0,), (0,)), ((), ())),
                           preferred_element_type=F32, precision=precision)


def _silu(z):
    return z * jax.nn.sigmoid(z)


def _split(x):
    hi = x.astype(BF16)
    lo = (x - hi.astype(F32)).astype(BF16)
    return hi, lo


def _cat_rows(a, b):
    return jnp.concatenate([a[0], b[0]], axis=0), jnp.concatenate([a[1], b[1]], axis=0)


_NN = (((1,), (0,)), ((), ()))
_NT = (((1,), (1,)), ((), ()))
_TN = (((0,), (0,)), ((), ()))


def _mm3(a, b, dims=_NN):
    d = lambda x, y: lax.dot_general(x, y, dims, preferred_element_type=F32)
    return d(a[0], b[0]) + d(a[0], b[1]) + d(a[1], b[0])


def _mm_exact_rhs(x, m_bf16):
    hi, lo = _split(x)
    return _dot(hi, m_bf16) + _dot(lo, m_bf16)


def _in_proj_kernel(x_ref, g_ref, w_ref, pa_ref, pb_ref, pc_ref):
    x = x_ref[...]
    ms = jnp.mean(x * x, axis=-1, keepdims=True)
    h = (x * lax.rsqrt(ms + RMS_EPS) * g_ref[...]).astype(BF16)
    p = _dot(h, w_ref[...])
    pa_ref[...] = p[:, :A_COLS]
    pb_ref[...] = p[:, A_COLS:A_COLS + B_COLS]
    pc_ref[...] = p[:, A_COLS + B_COLS:]


def _in_proj(x2d, g, w_bf16, tm):
    t, d = x2d.shape
    n = w_bf16.shape[1]
    assert t % tm == 0
    return pl.pallas_call(
        _in_proj_kernel,
        grid=(t // tm,),
        in_specs=[pl.BlockSpec((tm, d), lambda i: (i, 0)),
                  pl.BlockSpec((1, d), lambda i: (0, 0)),
                  pl.BlockSpec((d, n), lambda i: (0, 0))],
        out_specs=[pl.BlockSpec((tm, A_COLS), lambda i: (i, 0)),
                   pl.BlockSpec((tm, B_COLS), lambda i: (i, 0)),
                   pl.BlockSpec((tm, C_COLS), lambda i: (i, 0))],
        out_shape=[jax.ShapeDtypeStruct((t, A_COLS), F32),
                   jax.ShapeDtypeStruct((t, B_COLS), F32),
                   jax.ShapeDtypeStruct((t, C_COLS), F32)],
        compiler_params=_cparams(("parallel",)),
        name="in_proj",
    )(x2d, g, w_bf16)


def _gmlp_kernel(pa_ref, ws_ref, bs_ref, lng_ref, lnb_ref, gsum_ref, ya_ref, vn_ref,
                 *, n_sub):
    rows = pa_ref.shape[0]
    sub = rows // n_sub
    u = pa_ref[:, 0:A_WIDTH]
    v = pa_ref[:, A_WIDTH:2 * A_WIDTH]
    z = pa_ref[:, 2 * A_WIDTH:3 * A_WIDTH]
    gsum = gsum_ref[...]
    mu = _mm_exact_rhs(v, gsum) * (1.0 / HEAD_DIM)
    d = v - mu
    var = _mm_exact_rhs(d * d, gsum) * (1.0 / HEAD_DIM)
    vn = d * lax.rsqrt(var + LN_EPS) * lng_ref[...] + lnb_ref[...]
    vn_ref[...] = vn
    ri = lax.broadcasted_iota(jnp.int32, (sub, sub), 0)
    ci = lax.broadcasted_iota(jnp.int32, (sub, sub), 1)
    causal = ri >= ci
    vn_b = vn.astype(BF16)
    for h in range(A_HEADS):
        wm = jnp.where(causal, ws_ref[h], 0.0).astype(BF16)
        sl = slice(h * HEAD_DIM, (h + 1) * HEAD_DIM)
        for c in range(n_sub):
            rs = slice(c * sub, (c + 1) * sub)
            mixed = _dot(wm, vn_b[rs, sl]) + bs_ref[:, sl]
            ya_ref[rs, sl] = (u[rs, sl] * mixed * _silu(z[rs, sl])).astype(ya_ref.dtype)


def _gmlp(pa, ws_eff, bs_eff, lng, lnb, gsum, rows, n_sub):
    t = pa.shape[0]
    sub = rows // n_sub
    assert t % rows == 0 and ws_eff.shape == (A_HEADS, sub, sub)
    return pl.pallas_call(
        functools.partial(_gmlp_kernel, n_sub=n_sub),
        grid=(t // rows,),
        in_specs=[pl.BlockSpec((rows, A_COLS), lambda i: (i, 0)),
                  pl.BlockSpec((A_HEADS, sub, sub), lambda i: (0, 0, 0)),
                  pl.BlockSpec((sub, A_WIDTH), lambda i: (0, 0)),
                  pl.BlockSpec((1, A_WIDTH), lambda i: (0, 0)),
                  pl.BlockSpec((1, A_WIDTH), lambda i: (0, 0)),
                  pl.BlockSpec((A_WIDTH, A_WIDTH), lambda i: (0, 0))],
        out_specs=[pl.BlockSpec((rows, A_WIDTH), lambda i: (i, 0)),
                   pl.BlockSpec((rows, A_WIDTH), lambda i: (i, 0))],
        out_shape=[jax.ShapeDtypeStruct((t, A_WIDTH), BF16),
                   jax.ShapeDtypeStruct((t, A_WIDTH), F32)],
        compiler_params=_cparams(("parallel",)),
        name="gmlp",
    )(pa, ws_eff, bs_eff, lng, lnb, gsum)


def _rwkv_kernel(pb_ref, shift0_ref, wkv0_ref, mu_ref, w0_ref, w2_ref, a0_ref, a2_ref,
                 kk_ref, ka_ref, rk_ref, lng_ref, lnb_ref, gsum_ref,
                 yb_ref, wkv_ref, shift_ref, state_scr, prev_scr, o_scr, *, chunk):
    c = pl.program_id(1)
    n_c = pl.num_programs(1)
    rows = pb_ref.shape[0]
    n_ch = rows // chunk

    @pl.when(c == 0)
    def _():
        state_scr[...] = wkv0_ref[...]
        prev_scr[...] = shift0_ref[...]

    sb = pb_ref[:, 0:B_SHIFT_W]
    zb = pb_ref[:, B_SHIFT_W:B_COLS]
    row_id = lax.broadcasted_iota(jnp.int32, (rows, 1), 0)
    prev = jnp.where(row_id == 0, prev_scr[...], pltpu.roll(sb, 1, axis=0))
    last_row = sb[rows - 1:rows, :]
    prev_scr[...] = last_row
    xs = sb + (prev - sb) * mu_ref[...]
    r = xs[:, 0:B_WIDTH]
    k = xs[:, B_WIDTH:2 * B_WIDTH]
    v = xs[:, 2 * B_WIDTH:3 * B_WIDTH]
    wd = xs[:, 3 * B_WIDTH:3 * B_WIDTH + B_LORA]
    ad = xs[:, 3 * B_WIDTH + B_LORA:B_SHIFT_W]

    gsum = gsum_ref[...]
    y = -(w0_ref[...] + _dot(jnp.tanh(wd).astype(BF16), w2_ref[...].astype(BF16)))
    softplus = jnp.maximum(y, 0.0) + jnp.log(1.0 + jnp.exp(-jnp.abs(y)))
    logdecay = -jnp.exp(-softplus - 0.5)
    a = jax.nn.sigmoid(a0_ref[...] + _dot(ad.astype(BF16), a2_ref[...].astype(BF16)))
    kk = k * kk_ref[...]
    kk = kk * lax.rsqrt(jnp.maximum(_mm_exact_rhs(kk * kk, gsum), 1e-24))
    k2 = k * (1.0 + (a - 1.0) * ka_ref[...])

    ri = lax.broadcasted_iota(jnp.int32, (rows, rows), 0)
    ci = lax.broadcasted_iota(jnp.int32, (rows, rows), 1)
    same = (ri // chunk) == (ci // chunk)
    tri = jnp.where(same & (ri >= ci), 1.0, 0.0).astype(BF16)
    blk = jnp.where(same, 1.0, 0.0).astype(BF16)
    l1 = logdecay.astype(BF16)
    rem = logdecay - l1.astype(F32)
    l2 = rem.astype(BF16)
    l3 = (rem - l2.astype(F32)).astype(BF16)
    cum = _dot(tri, l1) + _dot(tri, l2) + _dot(tri, l3)
    tot = _dot(blk, l1) + _dot(blk, l2) + _dot(blk, l3)
    e_neg = jnp.exp(-cum)
    tail = jnp.exp(tot - cum)
    g_tot = jnp.exp(tot)
    kka = kk * a
    at = _split(-kk * jnp.exp(cum - logdecay))
    rt = _split(r * jnp.exp(cum))
    bp = _split(kka * e_neg)
    kp = _split(k2 * e_neg)
    bh = _split(kka * tail)
    kh = _split(k2 * tail)
    vs = _split(v)

    li = lax.broadcasted_iota(jnp.int32, (chunk, chunk), 0)
    lj = lax.broadcasted_iota(jnp.int32, (chunk, chunk), 1)
    lower2 = jnp.concatenate([li > lj, li >= lj], axis=0)
    eye = (li == lj).astype(F32)
    n_lvl = max(1, int(math.ceil(math.log2(chunk))))
    cut = lambda s, rs, cs: (s[0][rs, cs], s[1][rs, cs])

    hsl = [slice(h * HEAD_DIM, (h + 1) * HEAD_DIM) for h in range(B_HEADS)]
    rsl = [slice(ch * chunk, (ch + 1) * chunk) for ch in range(n_ch)]
    pairs = [(h, ch) for ch in range(n_ch) for h in range(B_HEADS)]
    at_h = {p: cut(at, rsl[p[1]], hsl[p[0]]) for p in pairs}
    rt_h = {p: cut(rt, rsl[p[1]], hsl[p[0]]) for p in pairs}
    v_h = {p: cut(vs, rsl[p[1]], hsl[p[0]]) for p in pairs}
    ar = {p: _cat_rows(at_h[p], rt_h[p]) for p in pairs}
    g1 = {p: jnp.where(lower2, _mm3(ar[p], cut(bp, rsl[p[1]], hsl[p[0]]), _NT), 0.0)
          for p in pairs}
    g2 = {p: jnp.where(lower2, _mm3(ar[p], cut(kp, rsl[p[1]], hsl[p[0]]), _NT), 0.0)
          for p in pairs}
    bq_v = {p: _mm3(_split(g2[p]), v_h[p]) for p in pairs}
    tinv = {p: eye + g1[p][:chunk] for p in pairs}
    if n_lvl > 1:
        a_s = {p: _split(g1[p][:chunk]) for p in pairs}
        pw = {p: _mm3(a_s[p], a_s[p]) for p in pairs}
        for lvl in range(1, n_lvl):
            p_s = {p: _split(pw[p]) for p in pairs}
            t_s = {p: _split(tinv[p]) for p in pairs}
            if lvl < n_lvl - 1:
                res = {p: _mm3(_cat_rows(p_s[p], t_s[p]), p_s[p]) for p in pairs}
                pw = {p: res[p][:chunk] for p in pairs}
                tinv = {p: tinv[p] + res[p][chunk:] for p in pairs}
            else:
                tinv = {p: tinv[p] + _mm3(t_s[p], p_s[p]) for p in pairs}
    t_s = {p: _split(tinv[p]) for p in pairs}
    w_m = {p: _mm3(t_s[p], at_h[p]) for p in pairs}
    y_m = {p: _mm3(t_s[p], _split(bq_v[p][:chunk])) for p in pairs}
    wr = {p: _cat_rows(_split(w_m[p]), rt_h[p]) for p in pairs}
    pm_s = {p: _split(g1[p][chunk:]) for p in pairs}
    bk = {p: _cat_rows(cut(bh, rsl[p[1]], hsl[p[0]]), cut(kh, rsl[p[1]], hsl[p[0]]))
          for p in pairs}

    state = [state_scr[h] for h in range(B_HEADS)]
    for ch in range(n_ch):
        ws = [_mm3(wr[(h, ch)], _split(state[h]), _NT) for h in range(B_HEADS)]
        u_s = [_split(ws[h][:chunk] + y_m[(h, ch)]) for h in range(B_HEADS)]
        for h in range(B_HEADS):
            p = (h, ch)
            state[h] = (state[h] * g_tot[ch * chunk:ch * chunk + 1, hsl[h]]
                        + _mm3(_cat_rows(u_s[h], v_h[p]), bk[p], _TN))
        for h in range(B_HEADS):
            p = (h, ch)
            o_scr[rsl[ch], hsl[h]] = ws[h][chunk:] + _mm3(pm_s[p], u_s[h]) + bq_v[p][chunk:]
    for h in range(B_HEADS):
        state_scr[h] = state[h]

    o = o_scr[...]
    m = _mm_exact_rhs(o, gsum) * (1.0 / HEAD_DIM)
    d = o - m
    var = _mm_exact_rhs(d * d, gsum) * (1.0 / HEAD_DIM)
    on = d * lax.rsqrt(var + GN_EPS) * lng_ref[...] + lnb_ref[...]
    bonus = _mm_exact_rhs(r * k2 * rk_ref[...], gsum) * v
    yb_ref[...] = ((on + bonus) * _silu(zb)).astype(yb_ref.dtype)

    @pl.when(c == n_c - 1)
    def _():
        wkv_ref[...] = state_scr[...]
        shift_ref[...] = last_row


def _rwkv(pb, shift0, wkv0, prm, gsum, n_seq, rows, chunk):
    t = pb.shape[0]
    n_c = t // (n_seq * rows)
    assert n_c * n_seq * rows == t and rows % chunk == 0
    vec = lambda w: pl.BlockSpec((1, w), lambda s, c: (0, 0))
    return pl.pallas_call(
        functools.partial(_rwkv_kernel, chunk=chunk),
        grid=(n_seq, n_c),
        in_specs=[pl.BlockSpec((rows, B_COLS), lambda s, c: (s * n_c + c, 0)),
                  pl.BlockSpec((None, 1, B_SHIFT_W), lambda s, c: (s, 0, 0)),
                  pl.BlockSpec((None, B_HEADS, HEAD_DIM, HEAD_DIM), lambda s, c: (s, 0, 0, 0)),
                  vec(B_SHIFT_W), vec(B_WIDTH),
                  pl.BlockSpec((B_LORA, B_WIDTH), lambda s, c: (0, 0)),
                  vec(B_WIDTH),
                  pl.BlockSpec((B_LORA, B_WIDTH), lambda s, c: (0, 0)),
                  vec(B_WIDTH), vec(B_WIDTH), vec(B_WIDTH), vec(B_WIDTH), vec(B_WIDTH),
                  pl.BlockSpec((B_WIDTH, B_WIDTH), lambda s, c: (0, 0))],
        out_specs=[pl.BlockSpec((rows, B_WIDTH), lambda s, c: (s * n_c + c, 0)),
                   pl.BlockSpec((None, B_HEADS, HEAD_DIM, HEAD_DIM), lambda s, c: (s, 0, 0, 0)),
                   pl.BlockSpec((None, 1, B_SHIFT_W), lambda s, c: (s, 0, 0))],
        out_shape=[jax.ShapeDtypeStruct((t, B_WIDTH), BF16),
                   jax.ShapeDtypeStruct((n_seq, B_HEADS, HEAD_DIM, HEAD_DIM), F32),
                   jax.ShapeDtypeStruct((n_seq, 1, B_SHIFT_W), F32)],
        scratch_shapes=[pltpu.VMEM((B_HEADS, HEAD_DIM, HEAD_DIM), F32),
                        pltpu.VMEM((1, B_SHIFT_W), F32),
                        pltpu.VMEM((rows, B_WIDTH), F32)],
        compiler_params=_cparams(("parallel", "arbitrary")),
        name="rwkv",
    )(pb, shift0, wkv0, prm["mu"], prm["w0"], prm["w2"], prm["a0"], prm["a2"],
      prm["kk"], prm["ka"], prm["rk"], prm["lnx_g"], prm["lnx_b"], gsum)


def _qk_norm_rope(x, g, cos, sin_signed, gsum):
    ms = _mm_exact_rhs(x * x, gsum) * (1.0 / HEAD_DIM)
    y = x * lax.rsqrt(ms + RMS_EPS) * g
    half = HEAD_DIM // 2
    lane = lax.broadcasted_iota(jnp.int32, (1, 128), 1)
    first = (lane % HEAD_DIM) < half
    outs = []
    for p in range(x.shape[1] // 128):
        yp = y[:, p * 128:(p + 1) * 128]
        partner = jnp.where(first, pltpu.roll(yp, 128 - half, axis=1),
                            pltpu.roll(yp, half, axis=1))
        outs.append(yp * cos + partner * sin_signed)
    return outs


def _prep_prompt_kernel(pc_ref, cos_ref, sin_ref, qg_ref, kg_ref, gsum_ref,
                        qt_ref, kr_ref, kt_ref, vt_ref, kmean_ref):
    gsum = gsum_ref[...]
    cos = cos_ref[...]
    sin = sin_ref[...]
    qs = _qk_norm_rope(pc_ref[:, 0:C_WIDTH], qg_ref[...], cos, sin, gsum)
    ks = _qk_norm_rope(pc_ref[:, C_WIDTH:2 * C_WIDTH], kg_ref[...], cos, sin, gsum)
    for p in range(C_HEADS // 2):
        vp = pc_ref[:, 2 * C_WIDTH + p * 128:2 * C_WIDTH + (p + 1) * 128]
        qpt = qs[p].T
        kpt = ks[p].T
        vpt = vp.T
        kmean_ref[:, p * 128:(p + 1) * 128] = jnp.mean(ks[p], axis=0, keepdims=True)
        for j in range(2):
            h = 2 * p + j
            qt_ref[h] = qpt[j * HEAD_DIM:(j + 1) * HEAD_DIM, :]
            kt_ref[h] = kpt[j * HEAD_DIM:(j + 1) * HEAD_DIM, :]
            vt_ref[h] = vpt[j * HEAD_DIM:(j + 1) * HEAD_DIM, :]
            kr_ref[h] = ks[p][:, j * HEAD_DIM:(j + 1) * HEAD_DIM].astype(kr_ref.dtype)


def _prep_prompt(pc, cos, sin, qg, kg, gsum, n_seq, seq_len):
    tm = MOBA_BLOCK
    n_t = seq_len // tm
    assert n_t * tm == seq_len
    tr = lambda: pl.BlockSpec((None, C_HEADS, HEAD_DIM, tm), lambda s, i: (s, 0, 0, i))
    return pl.pallas_call(
        _prep_prompt_kernel,
        grid=(n_seq, n_t),
        in_specs=[pl.BlockSpec((tm, C_COLS), lambda s, i: (s * n_t + i, 0)),
                  pl.BlockSpec((tm, 128), lambda s, i: (i, 0)),
                  pl.BlockSpec((tm, 128), lambda s, i: (i, 0)),
                  pl.BlockSpec((1, C_WIDTH), lambda s, i: (0, 0)),
                  pl.BlockSpec((1, C_WIDTH), lambda s, i: (0, 0)),
                  pl.BlockSpec((C_WIDTH, C_WIDTH), lambda s, i: (0, 0))],
        out_specs=[tr(),
                   pl.BlockSpec((None, C_HEADS, tm, HEAD_DIM), lambda s, i: (s, 0, i, 0)),
                   tr(), tr(),
                   pl.BlockSpec((None, None, 1, C_WIDTH), lambda s, i: (s, i, 0, 0))],
        out_shape=[jax.ShapeDtypeStruct((n_seq, C_HEADS, HEAD_DIM, seq_len), F32),
                   jax.ShapeDtypeStruct((n_seq, C_HEADS, seq_len, HEAD_DIM), BF16),
                   jax.ShapeDtypeStruct((n_seq, C_HEADS, HEAD_DIM, seq_len), F32),
                   jax.ShapeDtypeStruct((n_seq, C_HEADS, HEAD_DIM, seq_len), F32),
                   jax.ShapeDtypeStruct((n_seq, n_t, 1, C_WIDTH), F32)],
        compiler_params=_cparams(("parallel", "parallel")),
        name="moba_prep_prompt",
    )(pc, cos, sin, qg, kg, gsum)


def _prep_sample_kernel(pc_ref, cos_ref, sin_ref, qg_ref, kg_ref, gsum_ref, q_ref, k_ref):
    gsum = gsum_ref[...]
    cos = cos_ref[...]
    sin = sin_ref[...]
    qs = _qk_norm_rope(pc_ref[:, 0:C_WIDTH], qg_ref[...], cos, sin, gsum)
    ks = _qk_norm_rope(pc_ref[:, C_WIDTH:2 * C_WIDTH], kg_ref[...], cos, sin, gsum)
    for p in range(C_HEADS // 2):
        q_ref[:, p * 128:(p + 1) * 128] = qs[p]
        k_ref[:, p * 128:(p + 1) * 128] = ks[p]


def _prep_sample(pc, cos, sin, qg, kg, gsum, tm):
    t = pc.shape[0]
    assert t % tm == 0
    return pl.pallas_call(
        _prep_sample_kernel,
        grid=(t // tm,),
        in_specs=[pl.BlockSpec((tm, C_COLS), lambda i: (i, 0)),
                  pl.BlockSpec((tm, 128), lambda i: (i, 0)),
                  pl.BlockSpec((tm, 128), lambda i: (i, 0)),
                  pl.BlockSpec((1, C_WIDTH), lambda i: (0, 0)),
                  pl.BlockSpec((1, C_WIDTH), lambda i: (0, 0)),
                  pl.BlockSpec((C_WIDTH, C_WIDTH), lambda i: (0, 0))],
        out_specs=[pl.BlockSpec((tm, C_WIDTH), lambda i: (i, 0)),
                   pl.BlockSpec((tm, C_WIDTH), lambda i: (i, 0))],
        out_shape=[jax.ShapeDtypeStruct((t, C_WIDTH), F32),
                   jax.ShapeDtypeStruct((t, C_WIDTH), F32)],
        compiler_params=_cparams(("parallel",)),
        name="moba_prep_sample",
    )(pc, cos, sin, qg, kg, gsum)


def _topk_past_mask(gates, n_valid):
    n = len(gates)
    valid = [jnp.where(j < n_valid, 1.0, 0.0).astype(F32) for j in range(n)]
    sel = []
    for j in range(n):
        rank = jnp.zeros(gates[j].shape, F32)
        for j2 in range(n):
            if j2 == j:
                continue
            beats = (gates[j2] > gates[j]) if j2 > j else (gates[j2] >= gates[j])
            rank = rank + jnp.where(beats, valid[j2], 0.0)
        sel.append(jnp.where(rank < MOBA_TOPK, valid[j], 0.0))
    return sel


def _moba_prompt_kernel(qt_ref, kr_ref, vt_ref, kro_ref, vto_ref, kmean_ref, z_ref, y_ref,
                        m_scr, l_scr, acc_scr):
    i = pl.program_id(1)
    n_blk = kr_ref.shape[1] // MOBA_BLOCK
    tq = qt_ref.shape[2]
    scale = HEAD_DIM ** -0.5
    heads = range(C_HEADS)
    hsl = [slice(h * HEAD_DIM, (h + 1) * HEAD_DIM) for h in heads]

    own_blk = jnp.full((1, tq), i, jnp.int32)
    qt = [qt_ref[h] for h in heads]
    kmean = kmean_ref[...]
    gate = [_mm3(_split(kmean[:, hsl[h]]), _split(qt[h])) for h in heads]
    bias = []
    for h in heads:
        sel = _topk_past_mask([gate[h][b:b + 1, :] for b in range(n_blk)], own_blk)
        bias.append([jnp.where(sel[b] > 0.5, 0.0, NEG_BIG) for b in range(n_blk - 1)])
    qb = [(qt[h] * scale).astype(BF16) for h in heads]

    key = lax.broadcasted_iota(jnp.int32, (MOBA_BLOCK, tq), 0)
    qry = lax.broadcasted_iota(jnp.int32, (MOBA_BLOCK, tq), 1)
    s = [jnp.where(key <= qry, _dot(kro_ref[h], qb[h]), NEG_BIG) for h in heads]
    m = [s[h].max(axis=0, keepdims=True) for h in heads]
    p = [jnp.exp(s[h] - m[h]) for h in heads]
    pv = [_dot(vto_ref[h].astype(BF16), p[h].astype(BF16)) for h in heads]
    for h in heads:
        m_scr[h] = m[h]
        l_scr[h] = p[h].sum(axis=0, keepdims=True)
        acc_scr[h] = pv[h]

    for b in range(n_blk - 1):
        @pl.when(b < i)
        def _(b=b):
            ks = slice(b * MOBA_BLOCK, (b + 1) * MOBA_BLOCK)
            s = [_dot(kr_ref[h, ks, :], qb[h]) + bias[h][b] for h in heads]
            m_old = [m_scr[h] for h in heads]
            m_new = [jnp.maximum(m_old[h], s[h].max(axis=0, keepdims=True)) for h in heads]
            p = [jnp.exp(s[h] - m_new[h]) for h in heads]
            pv = [_dot(vt_ref[h, :, ks].astype(BF16), p[h].astype(BF16)) for h in heads]
            for h in heads:
                alpha = jnp.exp(m_old[h] - m_new[h])
                m_scr[h] = m_new[h]
                l_scr[h] = l_scr[h] * alpha + p[h].sum(axis=0, keepdims=True)
                acc_scr[h] = acc_scr[h] * alpha + pv[h]

    for pr in range(C_HEADS // 2):
        out_t = jnp.concatenate([acc_scr[2 * pr + j] / l_scr[2 * pr + j] for j in range(2)], axis=0)
        cs = slice(pr * 128, (pr + 1) * 128)
        y_ref[:, cs] = (out_t.T * _silu(z_ref[:, cs])).astype(y_ref.dtype)


def _moba_prompt(qt, kr, vt, kmean, pc, n_seq, seq_len):
    n_q = seq_len // MOBA_BLOCK
    z_col = 3 * C_WIDTH // C_WIDTH
    return pl.pallas_call(
        _moba_prompt_kernel,
        grid=(n_seq, n_q),
        in_specs=[pl.BlockSpec((None, C_HEADS, HEAD_DIM, MOBA_BLOCK), lambda s, i: (s, 0, 0, i)),
                  pl.BlockSpec((None, C_HEADS, seq_len, HEAD_DIM), lambda s, i: (s, 0, 0, 0)),
                  pl.BlockSpec((None, C_HEADS, HEAD_DIM, seq_len), lambda s, i: (s, 0, 0, 0)),
                  pl.BlockSpec((None, C_HEADS, MOBA_BLOCK, HEAD_DIM), lambda s, i: (s, 0, i, 0)),
                  pl.BlockSpec((None, C_HEADS, HEAD_DIM, MOBA_BLOCK), lambda s, i: (s, 0, 0, i)),
                  pl.BlockSpec((None, n_q, C_WIDTH), lambda s, i: (s, 0, 0)),
                  pl.BlockSpec((MOBA_BLOCK, C_WIDTH), lambda s, i: (s * n_q + i, z_col))],
        out_specs=pl.BlockSpec((MOBA_BLOCK, C_WIDTH), lambda s, i: (s * n_q + i, 0)),
        out_shape=jax.ShapeDtypeStruct((n_seq * seq_len, C_WIDTH), BF16),
        scratch_shapes=[pltpu.VMEM((C_HEADS, 1, MOBA_BLOCK), F32),
                        pltpu.VMEM((C_HEADS, 1, MOBA_BLOCK), F32),
                        pltpu.VMEM((C_HEADS, HEAD_DIM, MOBA_BLOCK), F32)],
        compiler_params=_cparams(("parallel", "arbitrary")),
        name="moba_prompt",
    )(qt, kr, vt, kr, vt, kmean, pc)


def _moba_sample_kernel(pt_ref, q_ref, kn_ref, pc_ref, *refs, n_pages):
    del pt_ref
    k_refs, v_refs, y_ref = refs[:n_pages], refs[n_pages:2 * n_pages], refs[2 * n_pages]
    s_len = q_ref.shape[0]
    page = k_refs[0].shape[2]
    per_blk = MOBA_BLOCK // page
    n_past = n_pages // per_blk
    scale = HEAD_DIM ** -0.5
    heads = range(C_HEADS)
    hsl = [slice(h * HEAD_DIM, (h + 1) * HEAD_DIM) for h in heads]

    q_s = [_split(q_ref[:, hsl[h]]) for h in heads]
    s_rows = []
    for h in heads:
        parts = [_split(k_refs[g][h]) for g in range(n_pages)]
        k_s = (jnp.concatenate([p[0] for p in parts], axis=1),
               jnp.concatenate([p[1] for p in parts], axis=1))
        s_rows.append(_mm3(q_s[h], k_s))
    s_all = jnp.concatenate(s_rows, axis=0)
    so = jnp.concatenate([_mm3(q_s[h], _split(kn_ref[:, hsl[h]]), _NT) for h in heads], axis=0)

    blocks = [s_all[:, b * MOBA_BLOCK:(b + 1) * MOBA_BLOCK] for b in range(n_past)]
    gates = [blk.sum(axis=-1, keepdims=True) * (1.0 / MOBA_BLOCK) for blk in blocks]
    sel = _topk_past_mask(gates, n_past)
    n_rows = C_HEADS * s_len
    row = lax.broadcasted_iota(jnp.int32, (n_rows, s_len), 0) % s_len
    col = lax.broadcasted_iota(jnp.int32, (n_rows, s_len), 1)
    so = jnp.where(col <= row, so * scale, NEG_BIG)
    pieces = [jnp.where(sel[b] > 0.5, blocks[b] * scale, NEG_BIG) for b in range(n_past)]
    m = so.max(axis=-1, keepdims=True)
    for b in range(n_past):
        m = jnp.maximum(m, pieces[b].max(axis=-1, keepdims=True))
    po = jnp.exp(so - m)
    den = po.sum(axis=-1, keepdims=True)
    probs = []
    for b in range(n_past):
        pb = jnp.exp(pieces[b] - m)
        den = den + pb.sum(axis=-1, keepdims=True)
        probs.append(pb.astype(BF16))
    p_all = jnp.concatenate(probs, axis=1)
    po = po.astype(BF16)

    for h in heads:
        rs = slice(h * s_len, (h + 1) * s_len)
        v_t = jnp.concatenate([v_refs[g][h].astype(BF16) for g in range(n_pages)], axis=1)
        vn = pc_ref[:, 2 * C_WIDTH + h * HEAD_DIM:2 * C_WIDTH + (h + 1) * HEAD_DIM]
        z = pc_ref[:, 3 * C_WIDTH + h * HEAD_DIM:3 * C_WIDTH + (h + 1) * HEAD_DIM]
        acc = _dot_nt(p_all[rs], v_t) + _dot(po[rs], vn.astype(BF16))
        y_ref[:, hsl[h]] = (acc / den[rs] * _silu(z)).astype(y_ref.dtype)


def _moba_sample(page_table, qn, kn, pc, cache_kt, cache_vt, layer, n_seq, s_len):
    n_pages = page_table.shape[1]
    page = cache_kt.shape[-1]
    assert MOBA_BLOCK % page == 0 and (n_pages * page) % MOBA_BLOCK == 0
    cache_spec = lambda g: pl.BlockSpec(
        (None, None, C_HEADS, HEAD_DIM, page), lambda s, pt: (layer, pt[s, g], 0, 0, 0))
    grid_spec = pltpu.PrefetchScalarGridSpec(
        num_scalar_prefetch=1,
        grid=(n_seq,),
        in_specs=[pl.BlockSpec((s_len, C_WIDTH), lambda s, pt: (s, 0)),
                  pl.BlockSpec((s_len, C_WIDTH), lambda s, pt: (s, 0)),
                  pl.BlockSpec((s_len, C_COLS), lambda s, pt: (s, 0))]
                 + [cache_spec(g) for g in range(n_pages)] * 2,
        out_specs=pl.BlockSpec((s_len, C_WIDTH), lambda s, pt: (s, 0)))
    return pl.pallas_call(
        functools.partial(_moba_sample_kernel, n_pages=n_pages),
        grid_spec=grid_spec,
        out_shape=jax.ShapeDtypeStruct((n_seq * s_len, C_WIDTH), BF16),
        compiler_params=_cparams(("parallel",)),
        name="moba_sample",
    )(page_table, qn, kn, pc, *([cache_kt] * n_pages), *([cache_vt] * n_pages))


def _out_proj_kernel(x_ref, ya_ref, yb_ref, yc_ref, w_ref, o_ref):
    acc = _dot(ya_ref[...], w_ref[0:A_WIDTH, :])
    acc = acc + _dot(yb_ref[...], w_ref[A_WIDTH:A_WIDTH + B_WIDTH, :])
    acc = acc + _dot(yc_ref[...], w_ref[A_WIDTH + B_WIDTH:, :])
    o_ref[...] = x_ref[...] + acc


def _out_proj(x2d, ya, yb, yc, w_bf16, tm):
    t, d = x2d.shape
    assert t % tm == 0
    row = lambda w: pl.BlockSpec((tm, w), lambda i: (i, 0))
    return pl.pallas_call(
        _out_proj_kernel,
        grid=(t // tm,),
        in_specs=[row(d), row(A_WIDTH), row(B_WIDTH), row(C_WIDTH),
                  pl.BlockSpec(w_bf16.shape, lambda i: (0, 0))],
        out_specs=row(d),
        out_shape=jax.ShapeDtypeStruct((t, d), F32),
        compiler_params=_cparams(("parallel",)),
        name="out_proj",
    )(x2d, ya, yb, yc, w_bf16)


def _group_sum_matrix(width):
    g = jnp.arange(width) // HEAD_DIM
    return (g[:, None] == g[None, :]).astype(BF16)


def _rope_tables(pos):
    half = HEAD_DIM // 2
    inv = ROPE_THETA ** (-jnp.arange(half, dtype=F32) / half)
    ang = pos.astype(F32)[:, None] * inv[None, :]
    cos = jnp.cos(ang)
    sin = jnp.sin(ang)
    return jnp.tile(cos, (1, 4)), jnp.tile(jnp.concatenate([-sin, sin], axis=1), (1, 2))


def _tile_heads(g, n_heads):
    return jnp.tile(g.reshape(1, HEAD_DIM), (1, n_heads))


def kernel(x_prompt, x_sample, cache_k, cache_v, state_wkv, state_shift, page_table, norm_g, w_in, w_out, a_ln_g, a_ln_b, a_ws, a_bs, b_mu, b_w0, b_w2, b_a0, b_a2, b_kk, b_ka, b_rk, b_lnx_g, b_lnx_b, c_qn_g, c_kn_g):
    depth = w_in.shape[0]
    n_p, seq_len, d_model = x_prompt.shape
    n_s, s_len, _ = x_sample.shape
    page = cache_k.shape[2]
    past_len = page_table.shape[1] * page
    assert seq_len % MOBA_BLOCK == 0 and seq_len % A_CHUNK == 0 and seq_len % RWKV_CHUNK == 0
    assert past_len % MOBA_BLOCK == 0 and s_len % 8 == 0 and s_len <= min(A_CHUNK, RWKV_CHUNK)
    n_blk = seq_len // MOBA_BLOCK

    gsum_a = _group_sum_matrix(A_WIDTH)
    gsum_b = _group_sum_matrix(B_WIDTH)
    cos_p, sin_p = _rope_tables(jnp.arange(seq_len))
    cos_s, sin_s = _rope_tables(past_len + jnp.arange(n_s * s_len) % s_len)
    cache_kt = jnp.transpose(cache_k, (0, 1, 3, 4, 2))
    cache_vt = jnp.transpose(cache_v, (0, 1, 3, 4, 2))
    w_in_b = w_in.astype(BF16)
    w_out_b = w_out.astype(BF16)
    zero_shift = jnp.zeros((n_p, 1, B_SHIFT_W), F32)
    zero_wkv = jnp.zeros((n_p, B_HEADS, HEAD_DIM, HEAD_DIM), F32)
    seqs_per_tile = A_CHUNK // s_len
    eye_tile = jnp.eye(seqs_per_tile, dtype=F32)

    hp = x_prompt.reshape(n_p * seq_len, d_model)
    hs = x_sample.reshape(n_s * s_len, d_model)
    outs = {k: [] for k in ("kp", "vp", "ks", "vs", "wkvp", "wkvs", "shp", "shs", "gvs")}
    for l in range(depth):
        g = norm_g[l].reshape(1, d_model)
        lng_a = a_ln_g[l].reshape(1, A_WIDTH)
        lnb_a = a_ln_b[l].reshape(1, A_WIDTH)
        bs_full = jnp.repeat(a_bs[l].T, HEAD_DIM, axis=1)
        ws_s = jnp.stack([jnp.kron(eye_tile, a_ws[l, h, :s_len, :s_len]) for h in range(A_HEADS)])
        bs_s = jnp.tile(bs_full[:s_len], (seqs_per_tile, 1))
        prm = dict(mu=b_mu[l].reshape(1, -1), w0=b_w0[l].reshape(1, -1), w2=b_w2[l],
                   a0=b_a0[l].reshape(1, -1), a2=b_a2[l], kk=b_kk[l].reshape(1, -1),
                   ka=b_ka[l].reshape(1, -1), rk=b_rk[l].reshape(1, -1),
                   lnx_g=b_lnx_g[l].reshape(1, -1), lnx_b=b_lnx_b[l].reshape(1, -1))
        qg = _tile_heads(c_qn_g[l], C_HEADS)
        kg = _tile_heads(c_kn_g[l], C_HEADS)

        pa, pb, pc = _in_proj(hp, g, w_in_b[l], min(256, hp.shape[0]))
        ya, _ = _gmlp(pa, a_ws[l], bs_full, lng_a, lnb_a, gsum_a, 4 * A_CHUNK, 4)
        yb, wkvp, shp = _rwkv(pb, zero_shift, zero_wkv, prm, gsum_b, n_p,
                              min(RWKV_ROWS, seq_len), RWKV_CHUNK)
        qt, kr, kt, vt, kmean = _prep_prompt(pc, cos_p, sin_p, qg, kg, gsum_b, n_p, seq_len)
        yc = _moba_prompt(qt, kr, vt, kmean.reshape(n_p, n_blk, C_WIDTH), pc, n_p, seq_len)
        hp = _out_proj(hp, ya, yb, yc, w_out_b[l], min(512, hp.shape[0]))
        outs["kp"].append(jnp.transpose(kt, (0, 3, 1, 2)))
        outs["vp"].append(jnp.transpose(vt, (0, 3, 1, 2)))
        outs["wkvp"].append(wkvp)
        outs["shp"].append(shp.reshape(n_p, B_SHIFT_W))

        pa, pb, pc = _in_proj(hs, g, w_in_b[l], min(256, hs.shape[0]))
        ya, gv = _gmlp(pa, ws_s, bs_s, lng_a, lnb_a, gsum_a, A_CHUNK, 1)
        yb, wkvs, shs = _rwkv(pb, state_shift[l].reshape(n_s, 1, B_SHIFT_W), state_wkv[l],
                              prm, gsum_b, n_s, s_len, s_len)
        qn, kn = _prep_sample(pc, cos_s, sin_s, qg, kg, gsum_b, min(256, hs.shape[0]))
        yc = _moba_sample(page_table, qn, kn, pc, cache_kt, cache_vt, l, n_s, s_len)
        hs = _out_proj(hs, ya, yb, yc, w_out_b[l], min(512, hs.shape[0]))
        outs["ks"].append(kn.reshape(n_s, s_len, C_HEADS, HEAD_DIM))
        outs["vs"].append(pc[:, 2 * C_WIDTH:3 * C_WIDTH].reshape(n_s, s_len, C_HEADS, HEAD_DIM))
        outs["wkvs"].append(wkvs)
        outs["shs"].append(shs.reshape(n_s, B_SHIFT_W))
        outs["gvs"].append(gv.reshape(n_s, s_len, A_WIDTH))

    st = {k: jnp.stack(v) for k, v in outs.items()}
    return (hp.reshape(n_p, seq_len, d_model), hs.reshape(n_s, s_len, d_model),
            st["kp"], st["vp"], st["ks"], st["vs"], st["wkvp"], st["wkvs"],
            st["shp"], st["shs"], st["gvs"])
```

```python
import functools
import math

import jax
import jax.numpy as jnp
from jax import lax
from jax.experimental import pallas as pl
from jax.experimental.pallas import tpu as pltpu

F32 = jnp.float32
BF16 = jnp.bfloat16
HI = lax.Precision.HIGHEST

HEAD_DIM = 64
A_HEADS = 4
A_WIDTH = A_HEADS * HEAD_DIM
A_CHUNK = 128
B_HEADS = 6
B_WIDTH = B_HEADS * HEAD_DIM
B_LORA = 64
B_SHIFT_W = 3 * B_WIDTH + 2 * B_LORA
C_HEADS = 6
C_WIDTH = C_HEADS * HEAD_DIM
MOBA_BLOCK = 256
MOBA_TOPK = 3
ROPE_THETA = 10000.0
A_COLS = 3 * A_WIDTH
B_COLS = B_SHIFT_W + B_WIDTH
C_COLS = 4 * C_WIDTH
RMS_EPS = 1e-6
LN_EPS = 1e-5
GN_EPS = 64e-5
NEG_BIG = -1e30
LOG2_E = 1.4426950408889634

VMEM_LIMIT_BYTES = 56 * 1024 * 1024
RWKV_CHUNK = 64
RWKV_ROWS = 256
RWKV_SAMPLE_SEQS = 8


def _cparams(sem):
    return pltpu.CompilerParams(dimension_semantics=sem,
                                vmem_limit_bytes=VMEM_LIMIT_BYTES)


def _dot(a, b, precision=None):
    return jnp.dot(a, b, preferred_element_type=F32, precision=precision)


def _dot_nt(a, b, precision=None):
    return lax.dot_general(a, b, (((1,), (1,)), ((), ())),
                           preferred_element_type=F32, precision=precision)


def _dot_tn(a, b, precision=None):
    return lax.dot_general(a, b, (((0,), (0,)), ((), ())),
                           preferred_element_type=F32, precision=precision)


def _silu(z):
    return z * jax.nn.sigmoid(z)


def _split(x):
    hi = x.astype(BF16)
    lo = (x - hi.astype(F32)).astype(BF16)
    return hi, lo


def _cat_rows(a, b):
    return jnp.concatenate([a[0], b[0]], axis=0), jnp.concatenate([a[1], b[1]], axis=0)


_NN = (((1,), (0,)), ((), ()))
_NT = (((1,), (1,)), ((), ()))
_TN = (((0,), (0,)), ((), ()))


def _mm3(a, b, dims=_NN):
    d = lambda x, y: lax.dot_general(x, y, dims, preferred_element_type=F32)
    return d(a[0], b[0]) + d(a[0], b[1]) + d(a[1], b[0])


def _mm_exact_rhs(x, m_bf16):
    hi, lo = _split(x)
    return _dot(hi, m_bf16) + _dot(lo, m_bf16)


def _in_proj_kernel(x_ref, g_ref, w_ref, pa_ref, pb_ref, pc_ref):
    x = x_ref[...]
    ms = jnp.mean(x * x, axis=-1, keepdims=True)
    h = (x * lax.rsqrt(ms + RMS_EPS) * g_ref[...]).astype(BF16)
    p = _dot(h, w_ref[...])
    pa_ref[...] = p[:, :A_COLS]
    pb_ref[...] = p[:, A_COLS:A_COLS + B_COLS]
    pc_ref[...] = p[:, A_COLS + B_COLS:]


def _in_proj(x2d, g, w_bf16, tm):
    t, d = x2d.shape
    n = w_bf16.shape[1]
    assert t % tm == 0
    return pl.pallas_call(
        _in_proj_kernel,
        grid=(t // tm,),
        in_specs=[pl.BlockSpec((tm, d), lambda i: (i, 0)),
                  pl.BlockSpec((1, d), lambda i: (0, 0)),
                  pl.BlockSpec((d, n), lambda i: (0, 0))],
        out_specs=[pl.BlockSpec((tm, A_COLS), lambda i: (i, 0)),
                   pl.BlockSpec((tm, B_COLS), lambda i: (i, 0)),
                   pl.BlockSpec((tm, C_COLS), lambda i: (i, 0))],
        out_shape=[jax.ShapeDtypeStruct((t, A_COLS), F32),
                   jax.ShapeDtypeStruct((t, B_COLS), F32),
                   jax.ShapeDtypeStruct((t, C_COLS), F32)],
        compiler_params=_cparams(("parallel",)),
        name="in_proj",
    )(x2d, g, w_bf16)


def _gmlp_kernel(pa_ref, ws_ref, bs_ref, lng_ref, lnb_ref, gsum_ref, ya_ref, vn_ref,
                 *, n_sub):
    rows = pa_ref.shape[0]
    sub = rows // n_sub
    u = pa_ref[:, 0:A_WIDTH]
    v = pa_ref[:, A_WIDTH:2 * A_WIDTH]
    z = pa_ref[:, 2 * A_WIDTH:3 * A_WIDTH]
    gsum = gsum_ref[...]
    mu = _mm_exact_rhs(v, gsum) * (1.0 / HEAD_DIM)
    d = v - mu
    var = _mm_exact_rhs(d * d, gsum) * (1.0 / HEAD_DIM)
    vn = d * lax.rsqrt(var + LN_EPS) * lng_ref[...] + lnb_ref[...]
    vn_ref[...] = vn
    ri = lax.broadcasted_iota(jnp.int32, (sub, sub), 0)
    ci = lax.broadcasted_iota(jnp.int32, (sub, sub), 1)
    causal = ri >= ci
    vn_b = vn.astype(BF16)
    for h in range(A_HEADS):
        wm = jnp.where(causal, ws_ref[h], 0.0).astype(BF16)
        sl = slice(h * HEAD_DIM, (h + 1) * HEAD_DIM)
        for c in range(n_sub):
            rs = slice(c * sub, (c + 1) * sub)
            mixed = _dot(wm, vn_b[rs, sl]) + bs_ref[:, sl]
            ya_ref[rs, sl] = (u[rs, sl] * mixed * _silu(z[rs, sl])).astype(ya_ref.dtype)


def _gmlp(pa, ws_eff, bs_eff, lng, lnb, gsum, rows, n_sub):
    t = pa.shape[0]
    sub = rows // n_sub
    assert t % rows == 0 and ws_eff.shape == (A_HEADS, sub, sub)
    return pl.pallas_call(
        functools.partial(_gmlp_kernel, n_sub=n_sub),
        grid=(t // rows,),
        in_specs=[pl.BlockSpec((rows, A_COLS), lambda i: (i, 0)),
                  pl.BlockSpec((A_HEADS, sub, sub), lambda i: (0, 0, 0)),
                  pl.BlockSpec((sub, A_WIDTH), lambda i: (0, 0)),
                  pl.BlockSpec((1, A_WIDTH), lambda i: (0, 0)),
                  pl.BlockSpec((1, A_WIDTH), lambda i: (0, 0)),
                  pl.BlockSpec((A_WIDTH, A_WIDTH), lambda i: (0, 0))],
        out_specs=[pl.BlockSpec((rows, A_WIDTH), lambda i: (i, 0)),
                   pl.BlockSpec((rows, A_WIDTH), lambda i: (i, 0))],
        out_shape=[jax.ShapeDtypeStruct((t, A_WIDTH), BF16),
                   jax.ShapeDtypeStruct((t, A_WIDTH), F32)],
        compiler_params=_cparams(("parallel",)),
        name="gmlp",
    )(pa, ws_eff, bs_eff, lng, lnb, gsum)


def _rwkv_kernel(pb_ref, shift0_ref, wkv0_ref, mu_ref, w0_ref, w2_ref, a0_ref, a2_ref,
                 kk_ref, ka_ref, rk_ref, lng_ref, lnb_ref, gsum_ref,
                 yb_ref, wkv_ref, shift_ref, state_scr, prev_scr, o_scr, *, chunk, n_sub):
    c = pl.program_id(1)
    n_c = pl.num_programs(1)
    rows = pb_ref.shape[0]
    sub_rows = rows // n_sub
    n_ch = sub_rows // chunk

    @pl.when(c == 0)
    def _():
        state_scr[...] = wkv0_ref[...]
        prev_scr[...] = shift0_ref[...]

    sb = pb_ref[:, 0:B_SHIFT_W]
    zb = pb_ref[:, B_SHIFT_W:B_COLS]
    row_id = lax.broadcasted_iota(jnp.int32, (rows, 1), 0)
    prev = pltpu.roll(sb, 1, axis=0)
    last_rows = []
    for q in range(n_sub):
        prev = jnp.where(row_id == q * sub_rows, prev_scr[q], prev)
        last_rows.append(sb[(q + 1) * sub_rows - 1:(q + 1) * sub_rows, :])
        prev_scr[q] = last_rows[q]
    xs = sb + (prev - sb) * mu_ref[...]
    r = xs[:, 0:B_WIDTH]
    k = xs[:, B_WIDTH:2 * B_WIDTH]
    v = xs[:, 2 * B_WIDTH:3 * B_WIDTH]
    wd = xs[:, 3 * B_WIDTH:3 * B_WIDTH + B_LORA]
    ad = xs[:, 3 * B_WIDTH + B_LORA:B_SHIFT_W]

    gsum = gsum_ref[...]
    y = -(w0_ref[...] + _dot(jnp.tanh(wd).astype(BF16), w2_ref[...].astype(BF16)))
    softplus = jnp.maximum(y, 0.0) + jnp.log(1.0 + jnp.exp(-jnp.abs(y)))
    logdecay = -jnp.exp(-softplus - 0.5)
    a = jax.nn.sigmoid(a0_ref[...] + _dot(ad.astype(BF16), a2_ref[...].astype(BF16)))
    kk = k * kk_ref[...]
    kk = kk * lax.rsqrt(jnp.maximum(_mm_exact_rhs(kk * kk, gsum), 1e-24))
    k2 = k * (1.0 + (a - 1.0) * ka_ref[...])

    ri = lax.broadcasted_iota(jnp.int32, (rows, rows), 0)
    ci = lax.broadcasted_iota(jnp.int32, (rows, rows), 1)
    same = (ri // chunk) == (ci // chunk)
    tri = jnp.where(same & (ri >= ci), 1.0, 0.0).astype(BF16)
    blk = jnp.where(same, 1.0, 0.0).astype(BF16)
    l1 = logdecay.astype(BF16)
    rem = logdecay - l1.astype(F32)
    l2 = rem.astype(BF16)
    l3 = (rem - l2.astype(F32)).astype(BF16)
    cum = _dot(tri, l1) + _dot(tri, l2) + _dot(tri, l3)
    tot = _dot(blk, l1) + _dot(blk, l2) + _dot(blk, l3)
    e_neg = jnp.exp(-cum)
    tail = jnp.exp(tot - cum)
    g_tot = jnp.exp(tot)
    kka = kk * a
    at = (-kk * jnp.exp(cum - logdecay)).astype(BF16)
    rt = (r * jnp.exp(cum)).astype(BF16)
    bp = (kka * e_neg).astype(BF16)
    kp = (k2 * e_neg).astype(BF16)
    bh = (kka * tail).astype(BF16)
    kh = (k2 * tail).astype(BF16)
    vb = v.astype(BF16)

    li = lax.broadcasted_iota(jnp.int32, (2 * chunk, 2 * chunk), 0)
    lj = lax.broadcasted_iota(jnp.int32, (2 * chunk, 2 * chunk), 1) % chunk
    lower4 = jnp.where(li < chunk, li - 1, li - chunk) >= lj
    xi = lax.broadcasted_iota(jnp.int32, (chunk, 2 * chunk), 0)
    xj = lax.broadcasted_iota(jnp.int32, (chunk, 2 * chunk), 1)
    left = xj < chunk
    eye_r = (xj - chunk == xi).astype(F32)
    n_lvl = max(1, int(math.ceil(math.log2(chunk))))
    mm = lambda x, y, dims=_NN: lax.dot_general(x, y, dims, preferred_element_type=F32)
    cat = lambda x, y: jnp.concatenate([x, y], axis=0)

    hsl = [slice(h * HEAD_DIM, (h + 1) * HEAD_DIM) for h in range(B_HEADS)]
    chains = [(h, q) for q in range(n_sub) for h in range(B_HEADS)]
    trips = [(h, q, ch) for ch in range(n_ch) for (h, q) in chains]
    rsl = lambda t: slice(t[1] * sub_rows + t[2] * chunk, t[1] * sub_rows + (t[2] + 1) * chunk)
    cut = lambda x, t: x[rsl(t), hsl[t[0]]]
    at_h = {t: cut(at, t) for t in trips}
    rt_h = {t: cut(rt, t) for t in trips}
    v_h = {t: cut(vb, t) for t in trips}
    zero_v = jnp.zeros((chunk, HEAD_DIM), BF16)
    g = {t: jnp.where(lower4, mm(cat(at_h[t], rt_h[t]), cat(cut(bp, t), cut(kp, t)), _NT), 0.0)
         for t in trips}
    bq_v = {t: mm(g[t].astype(BF16), cat(zero_v, v_h[t])) for t in trips}
    x = {t: jnp.where(left, g[t][:chunk], eye_r) for t in trips}
    for _ in range(n_lvl):
        res = {t: mm(x[t][:, :chunk].astype(BF16), x[t].astype(BF16)) for t in trips}
        x = {t: res[t] + jnp.where(left, 0.0, x[t]) for t in trips}
    t_b = {t: x[t][:, chunk:].astype(BF16) for t in trips}
    wy = {t: mm(t_b[t], jnp.concatenate([at_h[t].astype(F32), bq_v[t][:chunk]], axis=1)
                .astype(BF16)) for t in trips}
    y_m = {t: wy[t][:, HEAD_DIM:] for t in trips}
    wr = {t: cat(wy[t][:, :HEAD_DIM].astype(BF16), rt_h[t]) for t in trips}
    pm_b = {t: g[t][chunk:, :chunk].astype(BF16) for t in trips}
    bk = {t: cat(cut(bh, t), cut(kh, t)) for t in trips}

    state = {(h, q): state_scr[q, h] for (h, q) in chains}
    for ch in range(n_ch):
        ws = {cq: mm(wr[cq + (ch,)], state[cq].astype(BF16), _NT) for cq in chains}
        u_b = {cq: (ws[cq][:chunk] + y_m[cq + (ch,)]).astype(BF16) for cq in chains}
        for cq in chains:
            t = cq + (ch,)
            r0 = rsl(t).start
            state[cq] = (state[cq] * g_tot[r0:r0 + 1, hsl[cq[0]]]
                         + mm(cat(u_b[cq], v_h[t]), bk[t], _TN))
        for cq in chains:
            t = cq + (ch,)
            o_scr[rsl(t), hsl[cq[0]]] = ws[cq][chunk:] + mm(pm_b[t], u_b[cq]) + bq_v[t][chunk:]
    for (h, q) in chains:
        state_scr[q, h] = state[(h, q)]

    o = o_scr[...]
    m = _mm_exact_rhs(o, gsum) * (1.0 / HEAD_DIM)
    d = o - m
    var = _mm_exact_rhs(d * d, gsum) * (1.0 / HEAD_DIM)
    on = d * lax.rsqrt(var + GN_EPS) * lng_ref[...] + lnb_ref[...]
    bonus = _mm_exact_rhs(r * k2 * rk_ref[...], gsum) * v
    yb_ref[...] = ((on + bonus) * _silu(zb)).astype(yb_ref.dtype)

    @pl.when(c == n_c - 1)
    def _():
        wkv_ref[...] = state_scr[...]
        for q in range(n_sub):
            shift_ref[q] = last_rows[q]


def _rwkv(pb, shift0, wkv0, prm, gsum, n_seq, n_sub, sub_rows, chunk):
    t = pb.shape[0]
    rows = n_sub * sub_rows
    n_c = t // (n_seq * sub_rows)
    assert n_c * n_seq * sub_rows == t and sub_rows % chunk == 0 and n_seq % n_sub == 0
    assert n_sub == 1 or n_c == 1
    n_g = n_seq // n_sub
    vec = lambda w: pl.BlockSpec((1, w), lambda s, c: (0, 0))
    return pl.pallas_call(
        functools.partial(_rwkv_kernel, chunk=chunk, n_sub=n_sub),
        grid=(n_g, n_c),
        in_specs=[pl.BlockSpec((rows, B_COLS), lambda s, c: (s * n_c + c, 0)),
                  pl.BlockSpec((n_sub, 1, B_SHIFT_W), lambda s, c: (s, 0, 0)),
                  pl.BlockSpec((n_sub, B_HEADS, HEAD_DIM, HEAD_DIM), lambda s, c: (s, 0, 0, 0)),
                  vec(B_SHIFT_W), vec(B_WIDTH),
                  pl.BlockSpec((B_LORA, B_WIDTH), lambda s, c: (0, 0)),
                  vec(B_WIDTH),
                  pl.BlockSpec((B_LORA, B_WIDTH), lambda s, c: (0, 0)),
                  vec(B_WIDTH), vec(B_WIDTH), vec(B_WIDTH), vec(B_WIDTH), vec(B_WIDTH),
                  pl.BlockSpec((B_WIDTH, B_WIDTH), lambda s, c: (0, 0))],
        out_specs=[pl.BlockSpec((rows, B_WIDTH), lambda s, c: (s * n_c + c, 0)),
                   pl.BlockSpec((n_sub, B_HEADS, HEAD_DIM, HEAD_DIM), lambda s, c: (s, 0, 0, 0)),
                   pl.BlockSpec((n_sub, 1, B_SHIFT_W), lambda s, c: (s, 0, 0))],
        out_shape=[jax.ShapeDtypeStruct((t, B_WIDTH), BF16),
                   jax.ShapeDtypeStruct((n_seq, B_HEADS, HEAD_DIM, HEAD_DIM), F32),
                   jax.ShapeDtypeStruct((n_seq, 1, B_SHIFT_W), F32)],
        scratch_shapes=[pltpu.VMEM((n_sub, B_HEADS, HEAD_DIM, HEAD_DIM), F32),
                        pltpu.VMEM((n_sub, 1, B_SHIFT_W), F32),
                        pltpu.VMEM((rows, B_WIDTH), F32)],
        compiler_params=_cparams(("parallel", "arbitrary")),
        name="rwkv",
    )(pb, shift0, wkv0, prm["mu"], prm["w0"], prm["w2"], prm["a0"], prm["a2"],
      prm["kk"], prm["ka"], prm["rk"], prm["lnx_g"], prm["lnx_b"], gsum)


def _qk_norm_rope(x, g, cos, sin_signed, gsum):
    ms = _mm_exact_rhs(x * x, gsum) * (1.0 / HEAD_DIM)
    y = x * lax.rsqrt(ms + RMS_EPS) * g
    half = HEAD_DIM // 2
    lane = lax.broadcasted_iota(jnp.int32, (1, 128), 1)
    first = (lane % HEAD_DIM) < half
    outs = []
    for p in range(x.shape[1] // 128):
        yp = y[:, p * 128:(p + 1) * 128]
        partner = jnp.where(first, pltpu.roll(yp, 128 - half, axis=1),
                            pltpu.roll(yp, half, axis=1))
        outs.append(yp * cos + partner * sin_signed)
    return outs


def _prep_prompt_kernel(pc_ref, cos_ref, sin_ref, qg_ref, kg_ref, gsum_ref,
                        qt_ref, kr_ref, kt_ref, vt_ref, kmean_ref):
    gsum = gsum_ref[...]
    cos = cos_ref[...]
    sin = sin_ref[...]
    qs = _qk_norm_rope(pc_ref[:, 0:C_WIDTH], qg_ref[...], cos, sin, gsum)
    ks = _qk_norm_rope(pc_ref[:, C_WIDTH:2 * C_WIDTH], kg_ref[...], cos, sin, gsum)
    for p in range(C_HEADS // 2):
        vp = pc_ref[:, 2 * C_WIDTH + p * 128:2 * C_WIDTH + (p + 1) * 128]
        qpt = qs[p].T
        kpt = ks[p].T
        vpt = vp.T
        kmean_ref[:, p * 128:(p + 1) * 128] = jnp.mean(ks[p], axis=0, keepdims=True)
        for j in range(2):
            h = 2 * p + j
            qt_ref[h] = qpt[j * HEAD_DIM:(j + 1) * HEAD_DIM, :]
            kt_ref[h] = kpt[j * HEAD_DIM:(j + 1) * HEAD_DIM, :]
            vt_ref[h] = vpt[j * HEAD_DIM:(j + 1) * HEAD_DIM, :]
            kr_ref[h] = ks[p][:, j * HEAD_DIM:(j + 1) * HEAD_DIM].astype(kr_ref.dtype)


def _prep_prompt(pc, cos, sin, qg, kg, gsum, n_seq, seq_len):
    tm = MOBA_BLOCK
    n_t = seq_len // tm
    assert n_t * tm == seq_len
    tr = lambda: pl.BlockSpec((None, C_HEADS, HEAD_DIM, tm), lambda s, i: (s, 0, 0, i))
    return pl.pallas_call(
        _prep_prompt_kernel,
        grid=(n_seq, n_t),
        in_specs=[pl.BlockSpec((tm, 3 * C_WIDTH), lambda s, i: (s * n_t + i, 0)),
                  pl.BlockSpec((tm, 128), lambda s, i: (i, 0)),
                  pl.BlockSpec((tm, 128), lambda s, i: (i, 0)),
                  pl.BlockSpec((1, C_WIDTH), lambda s, i: (0, 0)),
                  pl.BlockSpec((1, C_WIDTH), lambda s, i: (0, 0)),
                  pl.BlockSpec((C_WIDTH, C_WIDTH), lambda s, i: (0, 0))],
        out_specs=[tr(),
                   pl.BlockSpec((None, C_HEADS, tm, HEAD_DIM), lambda s, i: (s, 0, i, 0)),
                   tr(), tr(),
                   pl.BlockSpec((None, None, 1, C_WIDTH), lambda s, i: (s, i, 0, 0))],
        out_shape=[jax.ShapeDtypeStruct((n_seq, C_HEADS, HEAD_DIM, seq_len), F32),
                   jax.ShapeDtypeStruct((n_seq, C_HEADS, seq_len, HEAD_DIM), BF16),
                   jax.ShapeDtypeStruct((n_seq, C_HEADS, HEAD_DIM, seq_len), F32),
                   jax.ShapeDtypeStruct((n_seq, C_HEADS, HEAD_DIM, seq_len), F32),
                   jax.ShapeDtypeStruct((n_seq, n_t, 1, C_WIDTH), F32)],
        compiler_params=_cparams(("parallel", "parallel")),
        name="moba_prep_prompt",
    )(pc, cos, sin, qg, kg, gsum)


def _prep_sample_kernel(pc_ref, cos_ref, sin_ref, qg_ref, kg_ref, gsum_ref, q_ref, k_ref):
    gsum = gsum_ref[...]
    cos = cos_ref[...]
    sin = sin_ref[...]
    qs = _qk_norm_rope(pc_ref[:, 0:C_WIDTH], qg_ref[...], cos, sin, gsum)
    ks = _qk_norm_rope(pc_ref[:, C_WIDTH:2 * C_WIDTH], kg_ref[...], cos, sin, gsum)
    for p in range(C_HEADS // 2):
        q_ref[:, p * 128:(p + 1) * 128] = qs[p]
        k_ref[:, p * 128:(p + 1) * 128] = ks[p]


def _prep_sample(pc, cos, sin, qg, kg, gsum, tm):
    t = pc.shape[0]
    assert t % tm == 0
    return pl.pallas_call(
        _prep_sample_kernel,
        grid=(t // tm,),
        in_specs=[pl.BlockSpec((tm, C_COLS), lambda i: (i, 0)),
                  pl.BlockSpec((tm, 128), lambda i: (i, 0)),
                  pl.BlockSpec((tm, 128), lambda i: (i, 0)),
                  pl.BlockSpec((1, C_WIDTH), lambda i: (0, 0)),
                  pl.BlockSpec((1, C_WIDTH), lambda i: (0, 0)),
                  pl.BlockSpec((C_WIDTH, C_WIDTH), lambda i: (0, 0))],
        out_specs=[pl.BlockSpec((tm, C_WIDTH), lambda i: (i, 0)),
                   pl.BlockSpec((tm, C_WIDTH), lambda i: (i, 0))],
        out_shape=[jax.ShapeDtypeStruct((t, C_WIDTH), F32),
                   jax.ShapeDtypeStruct((t, C_WIDTH), F32)],
        compiler_params=_cparams(("parallel",)),
        name="moba_prep_sample",
    )(pc, cos, sin, qg, kg, gsum)


def _topk_past_mask(gates, n_valid):
    n = len(gates)
    valid = [jnp.where(j < n_valid, 1.0, 0.0).astype(F32) for j in range(n)]
    sel = []
    for j in range(n):
        rank = jnp.zeros(gates[j].shape, F32)
        for j2 in range(n):
            if j2 == j:
                continue
            beats = (gates[j2] > gates[j]) if j2 > j else (gates[j2] >= gates[j])
            rank = rank + jnp.where(beats, valid[j2], 0.0)
        sel.append(jnp.where(rank < MOBA_TOPK, valid[j], 0.0))
    return sel


def _moba_prompt_kernel(qt_ref, kr_ref, vt_ref, kro_ref, vto_ref, kmean_ref, z_ref, y_ref,
                        m_scr, l_scr, acc_scr):
    i = pl.program_id(1)
    n_blk = kr_ref.shape[1] // MOBA_BLOCK
    tq = qt_ref.shape[2]
    scale = HEAD_DIM ** -0.5
    heads = range(C_HEADS)
    hsl = [slice(h * HEAD_DIM, (h + 1) * HEAD_DIM) for h in heads]

    own_blk = jnp.full((1, tq), i, jnp.int32)
    qt = [qt_ref[h] for h in heads]
    kmean = kmean_ref[...]
    gate = [_mm3(_split(kmean[:, hsl[h]]), _split(qt[h])) for h in heads]
    bias = []
    for h in heads:
        sel = _topk_past_mask([gate[h][b:b + 1, :] for b in range(n_blk)], own_blk)
        bias.append([jnp.where(sel[b] > 0.5, 0.0, NEG_BIG) for b in range(n_blk - 1)])
    qb = [(qt[h] * (scale * LOG2_E)).astype(BF16) for h in heads]

    key = lax.broadcasted_iota(jnp.int32, (MOBA_BLOCK, tq), 0)
    qry = lax.broadcasted_iota(jnp.int32, (MOBA_BLOCK, tq), 1)
    s = [jnp.where(key <= qry, _dot(kro_ref[h], qb[h]), NEG_BIG) for h in heads]
    m = [s[h].max(axis=0, keepdims=True) for h in heads]
    p = [jnp.exp2(s[h] - m[h]) for h in heads]
    pv = [_dot(vto_ref[h].astype(BF16), p[h].astype(BF16)) for h in heads]
    for h in heads:
        m_scr[h] = m[h]
        l_scr[h] = p[h].sum(axis=0, keepdims=True)
        acc_scr[h] = pv[h]

    for b in range(n_blk - 1):
        @pl.when(b < i)
        def _(b=b):
            ks = slice(b * MOBA_BLOCK, (b + 1) * MOBA_BLOCK)
            s = [_dot(kr_ref[h, ks, :], qb[h]) + bias[h][b] for h in heads]
            m_old = [m_scr[h] for h in heads]
            m_new = [jnp.maximum(m_old[h], s[h].max(axis=0, keepdims=True)) for h in heads]
            p = [jnp.exp2(s[h] - m_new[h]) for h in heads]
            pv = [_dot(vt_ref[h, :, ks].astype(BF16), p[h].astype(BF16)) for h in heads]
            for h in heads:
                alpha = jnp.exp2(m_old[h] - m_new[h])
                m_scr[h] = m_new[h]
                l_scr[h] = l_scr[h] * alpha + p[h].sum(axis=0, keepdims=True)
                acc_scr[h] = acc_scr[h] * alpha + pv[h]

    for pr in range(C_HEADS // 2):
        out_t = jnp.concatenate([acc_scr[2 * pr + j] / l_scr[2 * pr + j] for j in range(2)], axis=0)
        cs = slice(pr * 128, (pr + 1) * 128)
        y_ref[:, cs] = (out_t.T * _silu(z_ref[:, cs])).astype(y_ref.dtype)


def _moba_prompt(qt, kr, vt, kmean, pc, n_seq, seq_len):
    n_q = seq_len // MOBA_BLOCK
    z_col = 3 * C_WIDTH // C_WIDTH
    return pl.pallas_call(
        _moba_prompt_kernel,
        grid=(n_seq, n_q),
        in_specs=[pl.BlockSpec((None, C_HEADS, HEAD_DIM, MOBA_BLOCK), lambda s, i: (s, 0, 0, i)),
                  pl.BlockSpec((None, C_HEADS, seq_len, HEAD_DIM), lambda s, i: (s, 0, 0, 0)),
                  pl.BlockSpec((None, C_HEADS, HEAD_DIM, seq_len), lambda s, i: (s, 0, 0, 0)),
                  pl.BlockSpec((None, C_HEADS, MOBA_BLOCK, HEAD_DIM), lambda s, i: (s, 0, i, 0)),
                  pl.BlockSpec((None, C_HEADS, HEAD_DIM, MOBA_BLOCK), lambda s, i: (s, 0, 0, i)),
                  pl.BlockSpec((None, n_q, C_WIDTH), lambda s, i: (s, 0, 0)),
                  pl.BlockSpec((MOBA_BLOCK, C_WIDTH), lambda s, i: (s * n_q + i, z_col))],
        out_specs=pl.BlockSpec((MOBA_BLOCK, C_WIDTH), lambda s, i: (s * n_q + i, 0)),
        out_shape=jax.ShapeDtypeStruct((n_seq * seq_len, C_WIDTH), BF16),
        scratch_shapes=[pltpu.VMEM((C_HEADS, 1, MOBA_BLOCK), F32),
                        pltpu.VMEM((C_HEADS, 1, MOBA_BLOCK), F32),
                        pltpu.VMEM((C_HEADS, HEAD_DIM, MOBA_BLOCK), F32)],
        compiler_params=_cparams(("parallel", "arbitrary")),
        name="moba_prompt",
    )(qt, kr, vt, kr, vt, kmean, pc)


def _moba_sample_kernel(pt_ref, q_ref, kn_ref, pc_ref, *refs, n_pages):
    del pt_ref
    k_refs, v_refs, y_ref = refs[:n_pages], refs[n_pages:2 * n_pages], refs[2 * n_pages]
    s_len = q_ref.shape[0]
    page = k_refs[0].shape[2]
    per_blk = MOBA_BLOCK // page
    n_past = n_pages // per_blk
    scale = HEAD_DIM ** -0.5
    heads = range(C_HEADS)
    hsl = [slice(h * HEAD_DIM, (h + 1) * HEAD_DIM) for h in heads]

    q_s = [_split(q_ref[:, hsl[h]]) for h in heads]
    s_rows = []
    for h in heads:
        parts = [_split(k_refs[g][h]) for g in range(n_pages)]
        k_s = (jnp.concatenate([p[0] for p in parts], axis=1),
               jnp.concatenate([p[1] for p in parts], axis=1))
        s_rows.append(_mm3(q_s[h], k_s))
    s_all = jnp.concatenate(s_rows, axis=0)
    so = jnp.concatenate([_mm3(q_s[h], _split(kn_ref[:, hsl[h]]), _NT) for h in heads], axis=0)

    blocks = [s_all[:, b * MOBA_BLOCK:(b + 1) * MOBA_BLOCK] for b in range(n_past)]
    gates = [blk.sum(axis=-1, keepdims=True) * (1.0 / MOBA_BLOCK) for blk in blocks]
    sel = _topk_past_mask(gates, n_past)
    n_rows = C_HEADS * s_len
    row = lax.broadcasted_iota(jnp.int32, (n_rows, s_len), 0) % s_len
    col = lax.broadcasted_iota(jnp.int32, (n_rows, s_len), 1)
    so = jnp.where(col <= row, so * scale, NEG_BIG)
    pieces = [jnp.where(sel[b] > 0.5, blocks[b] * scale, NEG_BIG) for b in range(n_past)]
    m = so.max(axis=-1, keepdims=True)
    for b in range(n_past):
        m = jnp.maximum(m, pieces[b].max(axis=-1, keepdims=True))
    po = jnp.exp(so - m)
    den = po.sum(axis=-1, keepdims=True)
    probs = []
    for b in range(n_past):
        pb = jnp.exp(pieces[b] - m)
        den = den + pb.sum(axis=-1, keepdims=True)
        probs.append(pb.astype(BF16))
    p_all = jnp.concatenate(probs, axis=1)
    po = po.astype(BF16)

    for h in heads:
        rs = slice(h * s_len, (h + 1) * s_len)
        v_t = jnp.concatenate([v_refs[g][h].astype(BF16) for g in range(n_pages)], axis=1)
        vn = pc_ref[:, 2 * C_WIDTH + h * HEAD_DIM:2 * C_WIDTH + (h + 1) * HEAD_DIM]
        z = pc_ref[:, 3 * C_WIDTH + h * HEAD_DIM:3 * C_WIDTH + (h + 1) * HEAD_DIM]
        acc = _dot_nt(p_all[rs], v_t) + _dot(po[rs], vn.astype(BF16))
        y_ref[:, hsl[h]] = (acc / den[rs] * _silu(z)).astype(y_ref.dtype)


def _moba_sample(page_table, qn, kn, pc, cache_kt, cache_vt, layer, n_seq, s_len):
    n_pages = page_table.shape[1]
    page = cache_kt.shape[-1]
    assert MOBA_BLOCK % page == 0 and (n_pages * page) % MOBA_BLOCK == 0
    cache_spec = lambda g: pl.BlockSpec(
        (None, None, C_HEADS, HEAD_DIM, page), lambda s, pt: (layer, pt[s, g], 0, 0, 0))
    grid_spec = pltpu.PrefetchScalarGridSpec(
        num_scalar_prefetch=1,
        grid=(n_seq,),
        in_specs=[pl.BlockSpec((s_len, C_WIDTH), lambda s, pt: (s, 0)),
                  pl.BlockSpec((s_len, C_WIDTH), lambda s, pt: (s, 0)),
                  pl.BlockSpec((s_len, C_COLS), lambda s, pt: (s, 0))]
                 + [cache_spec(g) for g in range(n_pages)] * 2,
        out_specs=pl.BlockSpec((s_len, C_WIDTH), lambda s, pt: (s, 0)))
    return pl.pallas_call(
        functools.partial(_moba_sample_kernel, n_pages=n_pages),
        grid_spec=grid_spec,
        out_shape=jax.ShapeDtypeStruct((n_seq * s_len, C_WIDTH), BF16),
        compiler_params=_cparams(("parallel",)),
        name="moba_sample",
    )(page_table, qn, kn, pc, *([cache_kt] * n_pages), *([cache_vt] * n_pages))


def _out_proj_kernel(x_ref, ya_ref, yb_ref, yc_ref, w_ref, o_ref):
    acc = _dot(ya_ref[...], w_ref[0:A_WIDTH, :])
    acc = acc + _dot(yb_ref[...], w_ref[A_WIDTH:A_WIDTH + B_WIDTH, :])
    acc = acc + _dot(yc_ref[...], w_ref[A_WIDTH + B_WIDTH:, :])
    o_ref[...] = x_ref[...] + acc


def _out_proj(x2d, ya, yb, yc, w_bf16, tm):
    t, d = x2d.shape
    assert t % tm == 0
    row = lambda w: pl.BlockSpec((tm, w), lambda i: (i, 0))
    return pl.pallas_call(
        _out_proj_kernel,
        grid=(t // tm,),
        in_specs=[row(d), row(A_WIDTH), row(B_WIDTH), row(C_WIDTH),
                  pl.BlockSpec(w_bf16.shape, lambda i: (0, 0))],
        out_specs=row(d),
        out_shape=jax.ShapeDtypeStruct((t, d), F32),
        compiler_params=_cparams(("parallel",)),
        name="out_proj",
    )(x2d, ya, yb, yc, w_bf16)


def _group_sum_matrix(width):
    g = jnp.arange(width) // HEAD_DIM
    return (g[:, None] == g[None, :]).astype(BF16)


def _rope_tables(pos):
    half = HEAD_DIM // 2
    inv = ROPE_THETA ** (-jnp.arange(half, dtype=F32) / half)
    ang = pos.astype(F32)[:, None] * inv[None, :]
    cos = jnp.cos(ang)
    sin = jnp.sin(ang)
    return jnp.tile(cos, (1, 4)), jnp.tile(jnp.concatenate([-sin, sin], axis=1), (1, 2))


def _tile_heads(g, n_heads):
    return jnp.tile(g.reshape(1, HEAD_DIM), (1, n_heads))


def kernel(x_prompt, x_sample, cache_k, cache_v, state_wkv, state_shift, page_table, norm_g, w_in, w_out, a_ln_g, a_ln_b, a_ws, a_bs, b_mu, b_w0, b_w2, b_a0, b_a2, b_kk, b_ka, b_rk, b_lnx_g, b_lnx_b, c_qn_g, c_kn_g):
    depth = w_in.shape[0]
    n_p, seq_len, d_model = x_prompt.shape
    n_s, s_len, _ = x_sample.shape
    page = cache_k.shape[2]
    past_len = page_table.shape[1] * page
    assert seq_len % MOBA_BLOCK == 0 and seq_len % A_CHUNK == 0 and seq_len % RWKV_CHUNK == 0
    assert past_len % MOBA_BLOCK == 0 and s_len % 8 == 0 and s_len <= min(A_CHUNK, RWKV_CHUNK)
    n_blk = seq_len // MOBA_BLOCK

    gsum_a = _group_sum_matrix(A_WIDTH)
    gsum_b = _group_sum_matrix(B_WIDTH)
    cos_p, sin_p = _rope_tables(jnp.arange(seq_len))
    cos_s, sin_s = _rope_tables(past_len + jnp.arange(n_s * s_len) % s_len)
    cache_kt = jnp.transpose(cache_k, (0, 1, 3, 4, 2))
    cache_vt = jnp.transpose(cache_v, (0, 1, 3, 4, 2))
    w_in_b = w_in.astype(BF16)
    w_out_b = w_out.astype(BF16)
    zero_shift = jnp.zeros((n_p, 1, B_SHIFT_W), F32)
    zero_wkv = jnp.zeros((n_p, B_HEADS, HEAD_DIM, HEAD_DIM), F32)
    seqs_per_tile = A_CHUNK // s_len
    eye_tile = jnp.eye(seqs_per_tile, dtype=F32)

    hp = x_prompt.reshape(n_p * seq_len, d_model)
    hs = x_sample.reshape(n_s * s_len, d_model)
    outs = {k: [] for k in ("kp", "vp", "ks", "vs", "wkvp", "wkvs", "shp", "shs", "gvs")}
    for l in range(depth):
        g = norm_g[l].reshape(1, d_model)
        lng_a = a_ln_g[l].reshape(1, A_WIDTH)
        lnb_a = a_ln_b[l].reshape(1, A_WIDTH)
        bs_full = jnp.repeat(a_bs[l].T, HEAD_DIM, axis=1)
        ws_s = jnp.stack([jnp.kron(eye_tile, a_ws[l, h, :s_len, :s_len]) for h in range(A_HEADS)])
        bs_s = jnp.tile(bs_full[:s_len], (seqs_per_tile, 1))
        prm = dict(mu=b_mu[l].reshape(1, -1), w0=b_w0[l].reshape(1, -1), w2=b_w2[l],
                   a0=b_a0[l].reshape(1, -1), a2=b_a2[l], kk=b_kk[l].reshape(1, -1),
                   ka=b_ka[l].reshape(1, -1), rk=b_rk[l].reshape(1, -1),
                   lnx_g=b_lnx_g[l].reshape(1, -1), lnx_b=b_lnx_b[l].reshape(1, -1))
        qg = _tile_heads(c_qn_g[l], C_HEADS)
        kg = _tile_heads(c_kn_g[l], C_HEADS)

        pa, pb, pc = _in_proj(hp, g, w_in_b[l], min(256, hp.shape[0]))
        ya, _ = _gmlp(pa, a_ws[l], bs_full, lng_a, lnb_a, gsum_a, 4 * A_CHUNK, 4)
        yb, wkvp, shp = _rwkv(pb, zero_shift, zero_wkv, prm, gsum_b, n_p, 1,
                              min(RWKV_ROWS, seq_len), RWKV_CHUNK)
        qt, kr, kt, vt, kmean = _prep_prompt(pc, cos_p, sin_p, qg, kg, gsum_b, n_p, seq_len)
        yc = _moba_prompt(qt, kr, vt, kmean.reshape(n_p, n_blk, C_WIDTH), pc, n_p, seq_len)
        hp = _out_proj(hp, ya, yb, yc, w_out_b[l], min(512, hp.shape[0]))
        outs["kp"].append(jnp.transpose(kt, (0, 3, 1, 2)))
        outs["vp"].append(jnp.transpose(vt, (0, 3, 1, 2)))
        outs["wkvp"].append(wkvp)
        outs["shp"].append(shp.reshape(n_p, B_SHIFT_W))

        pa, pb, pc = _in_proj(hs, g, w_in_b[l], min(256, hs.shape[0]))
        ya, gv = _gmlp(pa, ws_s, bs_s, lng_a, lnb_a, gsum_a, A_CHUNK, 1)
        yb, wkvs, shs = _rwkv(pb, state_shift[l].reshape(n_s, 1, B_SHIFT_W), state_wkv[l],
                              prm, gsum_b, n_s, math.gcd(RWKV_SAMPLE_SEQS, n_s), s_len, s_len)
        qn, kn = _prep_sample(pc, cos_s, sin_s, qg, kg, gsum_b, min(256, hs.shape[0]))
        yc = _moba_sample(page_table, qn, kn, pc, cache_kt, cache_vt, l, n_s, s_len)
        hs = _out_proj(hs, ya, yb, yc, w_out_b[l], min(512, hs.shape[0]))
        outs["ks"].append(kn.reshape(n_s, s_len, C_HEADS, HEAD_DIM))
        outs["vs"].append(pc[:, 2 * C_WIDTH:3 * C_WIDTH].reshape(n_s, s_len, C_HEADS, HEAD_DIM))
        outs["wkvs"].append(wkvs)
        outs["shs"].append(shs.reshape(n_s, B_SHIFT_W))
        outs["gvs"].append(gv.reshape(n_s, s_len, A_WIDTH))

    st = {k: jnp.stack(v) for k, v in outs.items()}
    return (hp.reshape(n_p, seq_len, d_model), hs.reshape(n_s, s_len, d_model),
            st["kp"], st["vp"], st["ks"], st["vs"], st["wkvp"], st["wkvs"],
            st["shp"], st["shs"], st["gvs"])
```

```python
import functools
import math

import jax
import jax.numpy as jnp
from jax import lax
from jax.experimental import pallas as pl
from jax.experimental.pallas import tpu as pltpu

F32 = jnp.float32
BF16 = jnp.bfloat16
HI = lax.Precision.HIGHEST

LANES = 128
HEAD_DIM = 64
A_HEADS = 4
A_WIDTH = A_HEADS * HEAD_DIM
A_CHUNK = 128
B_HEADS = 6
B_WIDTH = B_HEADS * HEAD_DIM
B_LORA = 64
B_SHIFT_W = 3 * B_WIDTH + 2 * B_LORA
C_HEADS = 6
C_WIDTH = C_HEADS * HEAD_DIM
MOBA_BLOCK = 256
MOBA_TOPK = 3
ROPE_THETA = 10000.0
A_COLS = 3 * A_WIDTH
B_COLS = B_SHIFT_W + B_WIDTH
C_COLS = 4 * C_WIDTH
RMS_EPS = 1e-6
LN_EPS = 1e-5
GN_EPS = 64e-5
NEG_BIG = -1e30
LOG2_E = 1.4426950408889634

VMEM_LIMIT_BYTES = 56 * 1024 * 1024
RWKV_CHUNK = 64
RWKV_ROWS = 256
IN_PROJ_SUB = 128
MOBA_SAMPLE_SEQS = 2
RWKV_SAMPLE_SEQS = 8


def _cparams(sem):
    return pltpu.CompilerParams(dimension_semantics=sem,
                                vmem_limit_bytes=VMEM_LIMIT_BYTES)


def _dot(a, b, precision=None):
    return jnp.dot(a, b, preferred_element_type=F32, precision=precision)


def _dot_nt(a, b, precision=None):
    return lax.dot_general(a, b, (((1,), (1,)), ((), ())),
                           preferred_element_type=F32, precision=precision)


def _dot_tn(a, b, precision=None):
    return lax.dot_general(a, b, (((0,), (0,)), ((), ())),
                           preferred_element_type=F32, precision=precision)


def _silu(z):
    return z * jax.nn.sigmoid(z)


def _split(x):
    hi = x.astype(BF16)
    lo = (x - hi.astype(F32)).astype(BF16)
    return hi, lo


def _cat_rows(a, b):
    return jnp.concatenate([a[0], b[0]], axis=0), jnp.concatenate([a[1], b[1]], axis=0)


_NN = (((1,), (0,)), ((), ()))
_NT = (((1,), (1,)), ((), ()))
_TN = (((0,), (0,)), ((), ()))


def _mm3(a, b, dims=_NN):
    d = lambda x, y: lax.dot_general(x, y, dims, preferred_element_type=F32)
    return d(a[0], b[0]) + d(a[0], b[1]) + d(a[1], b[0])


def _mm_exact_rhs(x, g_bf16, passes=2):
    hi = x.astype(BF16)
    lo = (x - hi.astype(F32)).astype(BF16) if passes == 2 else None
    cols = []
    for p in range(x.shape[1] // LANES):
        cs = slice(p * LANES, (p + 1) * LANES)
        acc = _dot(hi[:, cs], g_bf16)
        if passes == 2:
            acc = acc + _dot(lo[:, cs], g_bf16)
        cols.append(acc)
    return jnp.concatenate(cols, axis=1)


def _in_proj_kernel(x_ref, g_ref, w_ref, pa_ref, pb_ref, pc_ref, *, n_sub):
    sub = x_ref.shape[0] // n_sub
    for j in range(n_sub):
        rs = slice(j * sub, (j + 1) * sub)
        x = x_ref[rs, :]
        ms = jnp.mean(x * x, axis=-1, keepdims=True)
        h = (x * lax.rsqrt(ms + RMS_EPS) * g_ref[...]).astype(BF16)
        p = _dot(h, w_ref[...])
        pa_ref[rs, :] = p[:, :A_COLS]
        pb_ref[rs, :] = p[:, A_COLS:A_COLS + B_COLS]
        pc_ref[rs, :] = p[:, A_COLS + B_COLS:]


def _in_proj(x2d, g, w_bf16, tm):
    t, d = x2d.shape
    n = w_bf16.shape[1]
    assert t % tm == 0 and tm % IN_PROJ_SUB == 0
    return pl.pallas_call(
        functools.partial(_in_proj_kernel, n_sub=tm // IN_PROJ_SUB),
        grid=(t // tm,),
        in_specs=[pl.BlockSpec((tm, d), lambda i: (i, 0)),
                  pl.BlockSpec((1, d), lambda i: (0, 0)),
                  pl.BlockSpec((d, n), lambda i: (0, 0))],
        out_specs=[pl.BlockSpec((tm, A_COLS), lambda i: (i, 0)),
                   pl.BlockSpec((tm, B_COLS), lambda i: (i, 0)),
                   pl.BlockSpec((tm, C_COLS), lambda i: (i, 0))],
        out_shape=[jax.ShapeDtypeStruct((t, A_COLS), F32),
                   jax.ShapeDtypeStruct((t, B_COLS), F32),
                   jax.ShapeDtypeStruct((t, C_COLS), F32)],
        compiler_params=_cparams(("parallel",)),
        name="in_proj",
    )(x2d, g, w_bf16)


def _gmlp_kernel(pa_ref, ws_ref, bs_ref, lng_ref, lnb_ref, gsum_ref, ya_ref, vn_ref,
                 *, n_sub):
    rows = pa_ref.shape[0]
    sub = rows // n_sub
    u = pa_ref[:, 0:A_WIDTH]
    v = pa_ref[:, A_WIDTH:2 * A_WIDTH]
    z = pa_ref[:, 2 * A_WIDTH:3 * A_WIDTH]
    gsum = gsum_ref[...]
    mu = _mm_exact_rhs(v, gsum) * (1.0 / HEAD_DIM)
    d = v - mu
    var = _mm_exact_rhs(d * d, gsum) * (1.0 / HEAD_DIM)
    vn = d * lax.rsqrt(var + LN_EPS) * lng_ref[...] + lnb_ref[...]
    vn_ref[...] = vn
    ri = lax.broadcasted_iota(jnp.int32, (sub, sub), 0)
    ci = lax.broadcasted_iota(jnp.int32, (sub, sub), 1)
    causal = ri >= ci
    vn_b = vn.astype(BF16)
    for h in range(A_HEADS):
        wm = jnp.where(causal, ws_ref[h], 0.0).astype(BF16)
        sl = slice(h * HEAD_DIM, (h + 1) * HEAD_DIM)
        for c in range(n_sub):
            rs = slice(c * sub, (c + 1) * sub)
            mixed = _dot(wm, vn_b[rs, sl]) + bs_ref[:, sl]
            ya_ref[rs, sl] = (u[rs, sl] * mixed * _silu(z[rs, sl])).astype(ya_ref.dtype)


def _gmlp(pa, ws_eff, bs_eff, lng, lnb, gsum, rows, n_sub):
    t = pa.shape[0]
    sub = rows // n_sub
    assert t % rows == 0 and ws_eff.shape == (A_HEADS, sub, sub)
    return pl.pallas_call(
        functools.partial(_gmlp_kernel, n_sub=n_sub),
        grid=(t // rows,),
        in_specs=[pl.BlockSpec((rows, A_COLS), lambda i: (i, 0)),
                  pl.BlockSpec((A_HEADS, sub, sub), lambda i: (0, 0, 0)),
                  pl.BlockSpec((sub, A_WIDTH), lambda i: (0, 0)),
                  pl.BlockSpec((1, A_WIDTH), lambda i: (0, 0)),
                  pl.BlockSpec((1, A_WIDTH), lambda i: (0, 0)),
                  pl.BlockSpec((LANES, LANES), lambda i: (0, 0))],
        out_specs=[pl.BlockSpec((rows, A_WIDTH), lambda i: (i, 0)),
                   pl.BlockSpec((rows, A_WIDTH), lambda i: (i, 0))],
        out_shape=[jax.ShapeDtypeStruct((t, A_WIDTH), BF16),
                   jax.ShapeDtypeStruct((t, A_WIDTH), F32)],
        compiler_params=_cparams(("parallel",)),
        name="gmlp",
    )(pa, ws_eff, bs_eff, lng, lnb, gsum)


def _rwkv_kernel(pb_ref, shift0_ref, wkv0_ref, mu_ref, w0_ref, w2_ref, a0_ref, a2_ref,
                 kk_ref, ka_ref, rk_ref, lng_ref, lnb_ref, gsum_ref,
                 yb_ref, wkv_ref, shift_ref, state_scr, prev_scr, o_scr, *, chunk, n_sub):
    c = pl.program_id(1)
    n_c = pl.num_programs(1)
    rows = pb_ref.shape[0]
    sub_rows = rows // n_sub
    n_ch = sub_rows // chunk

    @pl.when(c == 0)
    def _():
        state_scr[...] = wkv0_ref[...]
        prev_scr[...] = shift0_ref[...]

    sb = pb_ref[:, 0:B_SHIFT_W]
    zb = pb_ref[:, B_SHIFT_W:B_COLS]
    row_id = lax.broadcasted_iota(jnp.int32, (rows, 1), 0)
    prev = pltpu.roll(sb, 1, axis=0)
    last_rows = []
    for q in range(n_sub):
        prev = jnp.where(row_id == q * sub_rows, prev_scr[q], prev)
        last_rows.append(sb[(q + 1) * sub_rows - 1:(q + 1) * sub_rows, :])
        prev_scr[q] = last_rows[q]
    xs = sb + (prev - sb) * mu_ref[...]
    r = xs[:, 0:B_WIDTH]
    k = xs[:, B_WIDTH:2 * B_WIDTH]
    v = xs[:, 2 * B_WIDTH:3 * B_WIDTH]
    wd = xs[:, 3 * B_WIDTH:3 * B_WIDTH + B_LORA]
    ad = xs[:, 3 * B_WIDTH + B_LORA:B_SHIFT_W]

    gsum = gsum_ref[...]
    y = -(w0_ref[...] + _dot(jnp.tanh(wd).astype(BF16), w2_ref[...].astype(BF16)))
    softplus = jnp.maximum(y, 0.0) + jnp.log(1.0 + jnp.exp(-jnp.abs(y)))
    logdecay = -jnp.exp(-softplus - 0.5)
    a = jax.nn.sigmoid(a0_ref[...] + _dot(ad.astype(BF16), a2_ref[...].astype(BF16)))
    kk = k * kk_ref[...]
    kk = kk * lax.rsqrt(jnp.maximum(_mm_exact_rhs(kk * kk, gsum, 1), 1e-24))
    k2 = k * (1.0 + (a - 1.0) * ka_ref[...])

    ri = lax.broadcasted_iota(jnp.int32, (chunk, chunk), 0)
    ci = lax.broadcasted_iota(jnp.int32, (chunk, chunk), 1)
    tri = jnp.where(ri >= ci, 1.0, 0.0).astype(BF16)
    cums, tots = [], []
    for j in range(rows // chunk):
        ld = logdecay[j * chunk:(j + 1) * chunk]
        l1 = ld.astype(BF16)
        rem = ld - l1.astype(F32)
        l2 = rem.astype(BF16)
        l3 = (rem - l2.astype(F32)).astype(BF16)
        cj = _dot(tri, l1) + _dot(tri, l2) + _dot(tri, l3)
        cums.append(cj)
        tots.append(jnp.broadcast_to(cj[chunk - 1:chunk, :], cj.shape))
    cum = jnp.concatenate(cums, axis=0)
    tot = jnp.concatenate(tots, axis=0)
    e_neg = jnp.exp(-cum)
    tail = jnp.exp(tot - cum)
    g_tot = jnp.exp(tot)
    kka = kk * a
    at = (-kk * jnp.exp(cum - logdecay)).astype(BF16)
    rt = (r * jnp.exp(cum)).astype(BF16)
    bp = (kka * e_neg).astype(BF16)
    kp = (k2 * e_neg).astype(BF16)
    bh = (kka * tail).astype(BF16)
    kh = (k2 * tail).astype(BF16)
    vb = v.astype(BF16)

    li = lax.broadcasted_iota(jnp.int32, (2 * chunk, 2 * chunk), 0)
    lj = lax.broadcasted_iota(jnp.int32, (2 * chunk, 2 * chunk), 1) % chunk
    lower4 = jnp.where(li < chunk, li - 1, li - chunk) >= lj
    xi = lax.broadcasted_iota(jnp.int32, (chunk, 2 * chunk), 0)
    xj = lax.broadcasted_iota(jnp.int32, (chunk, 2 * chunk), 1)
    left = xj < chunk
    eye_r = (xj - chunk == xi).astype(F32)
    n_lvl = max(1, int(math.ceil(math.log2(chunk))))
    mm = lambda x, y, dims=_NN: lax.dot_general(x, y, dims, preferred_element_type=F32)
    cat = lambda x, y: jnp.concatenate([x, y], axis=0)

    hsl = [slice(h * HEAD_DIM, (h + 1) * HEAD_DIM) for h in range(B_HEADS)]
    chains = [(h, q) for q in range(n_sub) for h in range(B_HEADS)]
    trips = [(h, q, ch) for ch in range(n_ch) for (h, q) in chains]
    rsl = lambda t: slice(t[1] * sub_rows + t[2] * chunk, t[1] * sub_rows + (t[2] + 1) * chunk)
    cut = lambda x, t: x[rsl(t), hsl[t[0]]]
    at_h = {t: cut(at, t) for t in trips}
    rt_h = {t: cut(rt, t) for t in trips}
    v_h = {t: cut(vb, t) for t in trips}
    zero_v = jnp.zeros((chunk, HEAD_DIM), BF16)
    g = {t: jnp.where(lower4, mm(cat(at_h[t], rt_h[t]), cat(cut(bp, t), cut(kp, t)), _NT), 0.0)
         for t in trips}
    bq_v = {t: mm(g[t].astype(BF16), cat(zero_v, v_h[t])) for t in trips}
    x = {t: jnp.where(left, g[t][:chunk], eye_r) for t in trips}
    for _ in range(n_lvl):
        res = {t: mm(x[t][:, :chunk].astype(BF16), x[t].astype(BF16)) for t in trips}
        x = {t: res[t] + jnp.where(left, 0.0, x[t]) for t in trips}
    t_b = {t: x[t][:, chunk:].astype(BF16) for t in trips}
    wy = {t: mm(t_b[t], jnp.concatenate([at_h[t].astype(F32), bq_v[t][:chunk]], axis=1)
                .astype(BF16)) for t in trips}
    y_m = {t: wy[t][:, HEAD_DIM:] for t in trips}
    wr = {t: cat(wy[t][:, :HEAD_DIM].astype(BF16), rt_h[t]) for t in trips}
    pm_b = {t: g[t][chunk:, :chunk].astype(BF16) for t in trips}
    bk = {t: cat(cut(bh, t), cut(kh, t)) for t in trips}

    state = {(h, q): state_scr[q, h] for (h, q) in chains}
    for ch in range(n_ch):
        ws = {cq: mm(wr[cq + (ch,)], state[cq].astype(BF16), _NT) for cq in chains}
        u_b = {cq: (ws[cq][:chunk] + y_m[cq + (ch,)]).astype(BF16) for cq in chains}
        for cq in chains:
            t = cq + (ch,)
            r0 = rsl(t).start
            state[cq] = (state[cq] * g_tot[r0:r0 + 1, hsl[cq[0]]]
                         + mm(cat(u_b[cq], v_h[t]), bk[t], _TN))
        for cq in chains:
            t = cq + (ch,)
            o_scr[rsl(t), hsl[cq[0]]] = ws[cq][chunk:] + mm(pm_b[t], u_b[cq]) + bq_v[t][chunk:]
    for (h, q) in chains:
        state_scr[q, h] = state[(h, q)]

    o = o_scr[...]
    m = _mm_exact_rhs(o, gsum, 1) * (1.0 / HEAD_DIM)
    d = o - m
    var = _mm_exact_rhs(d * d, gsum, 1) * (1.0 / HEAD_DIM)
    on = d * lax.rsqrt(var + GN_EPS) * lng_ref[...] + lnb_ref[...]
    bonus = _mm_exact_rhs(r * k2 * rk_ref[...], gsum, 1) * v
    yb_ref[...] = ((on + bonus) * _silu(zb)).astype(yb_ref.dtype)

    @pl.when(c == n_c - 1)
    def _():
        wkv_ref[...] = state_scr[...]
        for q in range(n_sub):
            shift_ref[q] = last_rows[q]


def _rwkv(pb, shift0, wkv0, prm, gsum, n_seq, n_sub, sub_rows, chunk):
    t = pb.shape[0]
    rows = n_sub * sub_rows
    n_c = t // (n_seq * sub_rows)
    assert n_c * n_seq * sub_rows == t and sub_rows % chunk == 0 and n_seq % n_sub == 0
    assert n_sub == 1 or n_c == 1
    n_g = n_seq // n_sub
    vec = lambda w: pl.BlockSpec((1, w), lambda s, c: (0, 0))
    return pl.pallas_call(
        functools.partial(_rwkv_kernel, chunk=chunk, n_sub=n_sub),
        grid=(n_g, n_c),
        in_specs=[pl.BlockSpec((rows, B_COLS), lambda s, c: (s * n_c + c, 0)),
                  pl.BlockSpec((n_sub, 1, B_SHIFT_W), lambda s, c: (s, 0, 0)),
                  pl.BlockSpec((n_sub, B_HEADS, HEAD_DIM, HEAD_DIM), lambda s, c: (s, 0, 0, 0)),
                  vec(B_SHIFT_W), vec(B_WIDTH),
                  pl.BlockSpec((B_LORA, B_WIDTH), lambda s, c: (0, 0)),
                  vec(B_WIDTH),
                  pl.BlockSpec((B_LORA, B_WIDTH), lambda s, c: (0, 0)),
                  vec(B_WIDTH), vec(B_WIDTH), vec(B_WIDTH), vec(B_WIDTH), vec(B_WIDTH),
                  pl.BlockSpec((LANES, LANES), lambda s, c: (0, 0))],
        out_specs=[pl.BlockSpec((rows, B_WIDTH), lambda s, c: (s * n_c + c, 0)),
                   pl.BlockSpec((n_sub, B_HEADS, HEAD_DIM, HEAD_DIM), lambda s, c: (s, 0, 0, 0)),
                   pl.BlockSpec((n_sub, 1, B_SHIFT_W), lambda s, c: (s, 0, 0))],
        out_shape=[jax.ShapeDtypeStruct((t, B_WIDTH), BF16),
                   jax.ShapeDtypeStruct((n_seq, B_HEADS, HEAD_DIM, HEAD_DIM), F32),
                   jax.ShapeDtypeStruct((n_seq, 1, B_SHIFT_W), F32)],
        scratch_shapes=[pltpu.VMEM((n_sub, B_HEADS, HEAD_DIM, HEAD_DIM), F32),
                        pltpu.VMEM((n_sub, 1, B_SHIFT_W), F32),
                        pltpu.VMEM((rows, B_WIDTH), F32)],
        compiler_params=_cparams(("parallel", "arbitrary")),
        name="rwkv",
    )(pb, shift0, wkv0, prm["mu"], prm["w0"], prm["w2"], prm["a0"], prm["a2"],
      prm["kk"], prm["ka"], prm["rk"], prm["lnx_g"], prm["lnx_b"], gsum)


def _qk_norm_rope(x, g, cos, sin_signed, gsum):
    ms = _mm_exact_rhs(x * x, gsum) * (1.0 / HEAD_DIM)
    y = x * lax.rsqrt(ms + RMS_EPS) * g
    half = HEAD_DIM // 2
    lane = lax.broadcasted_iota(jnp.int32, (1, 128), 1)
    first = (lane % HEAD_DIM) < half
    outs = []
    for p in range(x.shape[1] // 128):
        yp = y[:, p * 128:(p + 1) * 128]
        partner = jnp.where(first, pltpu.roll(yp, 128 - half, axis=1),
                            pltpu.roll(yp, half, axis=1))
        outs.append(yp * cos + partner * sin_signed)
    return outs


def _prep_prompt_kernel(pc_ref, cos_ref, sin_ref, qg_ref, kg_ref, gsum_ref,
                        qt_ref, kr_ref, kt_ref, vt_ref, kmean_ref):
    gsum = gsum_ref[...]
    cos = cos_ref[...]
    sin = sin_ref[...]
    qs = _qk_norm_rope(pc_ref[:, 0:C_WIDTH], qg_ref[...], cos, sin, gsum)
    ks = _qk_norm_rope(pc_ref[:, C_WIDTH:2 * C_WIDTH], kg_ref[...], cos, sin, gsum)
    for p in range(C_HEADS // 2):
        vp = pc_ref[:, 2 * C_WIDTH + p * 128:2 * C_WIDTH + (p + 1) * 128]
        qpt = qs[p].T
        kpt = ks[p].T
        vpt = vp.T
        kmean_ref[:, p * 128:(p + 1) * 128] = jnp.mean(ks[p], axis=0, keepdims=True)
        for j in range(2):
            h = 2 * p + j
            qt_ref[h] = qpt[j * HEAD_DIM:(j + 1) * HEAD_DIM, :]
            kt_ref[h] = kpt[j * HEAD_DIM:(j + 1) * HEAD_DIM, :]
            vt_ref[h] = vpt[j * HEAD_DIM:(j + 1) * HEAD_DIM, :]
            kr_ref[h] = ks[p][:, j * HEAD_DIM:(j + 1) * HEAD_DIM].astype(kr_ref.dtype)


def _prep_prompt(pc, cos, sin, qg, kg, gsum, n_seq, seq_len):
    tm = MOBA_BLOCK
    n_t = seq_len // tm
    assert n_t * tm == seq_len
    tr = lambda: pl.BlockSpec((None, C_HEADS, HEAD_DIM, tm), lambda s, i: (s, 0, 0, i))
    return pl.pallas_call(
        _prep_prompt_kernel,
        grid=(n_seq, n_t),
        in_specs=[pl.BlockSpec((tm, 3 * C_WIDTH), lambda s, i: (s * n_t + i, 0)),
                  pl.BlockSpec((tm, 128), lambda s, i: (i, 0)),
                  pl.BlockSpec((tm, 128), lambda s, i: (i, 0)),
                  pl.BlockSpec((1, C_WIDTH), lambda s, i: (0, 0)),
                  pl.BlockSpec((1, C_WIDTH), lambda s, i: (0, 0)),
                  pl.BlockSpec((LANES, LANES), lambda s, i: (0, 0))],
        out_specs=[tr(),
                   pl.BlockSpec((None, C_HEADS, tm, HEAD_DIM), lambda s, i: (s, 0, i, 0)),
                   tr(), tr(),
                   pl.BlockSpec((None, None, 1, C_WIDTH), lambda s, i: (s, i, 0, 0))],
        out_shape=[jax.ShapeDtypeStruct((n_seq, C_HEADS, HEAD_DIM, seq_len), F32),
                   jax.ShapeDtypeStruct((n_seq, C_HEADS, seq_len, HEAD_DIM), BF16),
                   jax.ShapeDtypeStruct((n_seq, C_HEADS, HEAD_DIM, seq_len), F32),
                   jax.ShapeDtypeStruct((n_seq, C_HEADS, HEAD_DIM, seq_len), F32),
                   jax.ShapeDtypeStruct((n_seq, n_t, 1, C_WIDTH), F32)],
        compiler_params=_cparams(("parallel", "parallel")),
        name="moba_prep_prompt",
    )(pc, cos, sin, qg, kg, gsum)


def _prep_sample_kernel(pc_ref, cos_ref, sin_ref, qg_ref, kg_ref, gsum_ref, q_ref, k_ref):
    gsum = gsum_ref[...]
    cos = cos_ref[...]
    sin = sin_ref[...]
    qs = _qk_norm_rope(pc_ref[:, 0:C_WIDTH], qg_ref[...], cos, sin, gsum)
    ks = _qk_norm_rope(pc_ref[:, C_WIDTH:2 * C_WIDTH], kg_ref[...], cos, sin, gsum)
    for p in range(C_HEADS // 2):
        q_ref[:, p * 128:(p + 1) * 128] = qs[p]
        k_ref[:, p * 128:(p + 1) * 128] = ks[p]


def _prep_sample(pc, cos, sin, qg, kg, gsum, tm):
    t = pc.shape[0]
    assert t % tm == 0
    return pl.pallas_call(
        _prep_sample_kernel,
        grid=(t // tm,),
        in_specs=[pl.BlockSpec((tm, C_COLS), lambda i: (i, 0)),
                  pl.BlockSpec((tm, 128), lambda i: (i, 0)),
                  pl.BlockSpec((tm, 128), lambda i: (i, 0)),
                  pl.BlockSpec((1, C_WIDTH), lambda i: (0, 0)),
                  pl.BlockSpec((1, C_WIDTH), lambda i: (0, 0)),
                  pl.BlockSpec((LANES, LANES), lambda i: (0, 0))],
        out_specs=[pl.BlockSpec((tm, C_WIDTH), lambda i: (i, 0)),
                   pl.BlockSpec((tm, C_WIDTH), lambda i: (i, 0))],
        out_shape=[jax.ShapeDtypeStruct((t, C_WIDTH), F32),
                   jax.ShapeDtypeStruct((t, C_WIDTH), F32)],
        compiler_params=_cparams(("parallel",)),
        name="moba_prep_sample",
    )(pc, cos, sin, qg, kg, gsum)


def _topk_past_mask(gates, n_valid):
    n = len(gates)
    valid = [jnp.where(j < n_valid, 1.0, 0.0).astype(F32) for j in range(n)]
    sel = []
    for j in range(n):
        rank = jnp.zeros(gates[j].shape, F32)
        for j2 in range(n):
            if j2 == j:
                continue
            beats = (gates[j2] > gates[j]) if j2 > j else (gates[j2] >= gates[j])
            rank = rank + jnp.where(beats, valid[j2], 0.0)
        sel.append(jnp.where(rank < MOBA_TOPK, valid[j], 0.0))
    return sel


def _moba_prompt_kernel(qt_ref, kr_ref, vt_ref, kro_ref, vto_ref, kmean_ref, z_ref, y_ref,
                        m_scr, l_scr, acc_scr):
    i = pl.program_id(1)
    n_blk = kr_ref.shape[1] // MOBA_BLOCK
    tq = qt_ref.shape[2]
    scale = HEAD_DIM ** -0.5
    heads = range(C_HEADS)
    hsl = [slice(h * HEAD_DIM, (h + 1) * HEAD_DIM) for h in heads]

    own_blk = jnp.full((1, tq), i, jnp.int32)
    qt = [qt_ref[h] for h in heads]
    kmean = kmean_ref[...]
    gate = [_mm3(_split(kmean[:, hsl[h]]), _split(qt[h])) for h in heads]
    bias = []
    for h in heads:
        sel = _topk_past_mask([gate[h][b:b + 1, :] for b in range(n_blk)], own_blk)
        bias.append([jnp.where(sel[b] > 0.5, 0.0, NEG_BIG) for b in range(n_blk - 1)])
    qb = [(qt[h] * (scale * LOG2_E)).astype(BF16) for h in heads]

    key = lax.broadcasted_iota(jnp.int32, (MOBA_BLOCK, tq), 0)
    qry = lax.broadcasted_iota(jnp.int32, (MOBA_BLOCK, tq), 1)
    s = [jnp.where(key <= qry, _dot(kro_ref[h], qb[h]), NEG_BIG) for h in heads]
    m = [s[h].max(axis=0, keepdims=True) for h in heads]
    p = [jnp.exp2(s[h] - m[h]) for h in heads]
    pv = [_dot(vto_ref[h].astype(BF16), p[h].astype(BF16)) for h in heads]
    for h in heads:
        m_scr[h] = m[h]
        l_scr[h] = p[h].sum(axis=0, keepdims=True)
        acc_scr[h] = pv[h]

    for b in range(n_blk - 1):
        @pl.when(b < i)
        def _(b=b):
            ks = slice(b * MOBA_BLOCK, (b + 1) * MOBA_BLOCK)
            s = [_dot(kr_ref[h, ks, :], qb[h]) + bias[h][b] for h in heads]
            m_old = [m_scr[h] for h in heads]
            m_new = [jnp.maximum(m_old[h], s[h].max(axis=0, keepdims=True)) for h in heads]
            p = [jnp.exp2(s[h] - m_new[h]) for h in heads]
            pv = [_dot(vt_ref[h, :, ks].astype(BF16), p[h].astype(BF16)) for h in heads]
            for h in heads:
                alpha = jnp.exp2(m_old[h] - m_new[h])
                m_scr[h] = m_new[h]
                l_scr[h] = l_scr[h] * alpha + p[h].sum(axis=0, keepdims=True)
                acc_scr[h] = acc_scr[h] * alpha + pv[h]

    for pr in range(C_HEADS // 2):
        out_t = jnp.concatenate([acc_scr[2 * pr + j] / l_scr[2 * pr + j] for j in range(2)], axis=0)
        cs = slice(pr * 128, (pr + 1) * 128)
        y_ref[:, cs] = (out_t.T * _silu(z_ref[:, cs])).astype(y_ref.dtype)


def _moba_prompt(qt, kr, vt, kmean, pc, n_seq, seq_len):
    n_q = seq_len // MOBA_BLOCK
    z_col = 3 * C_WIDTH // C_WIDTH
    return pl.pallas_call(
        _moba_prompt_kernel,
        grid=(n_seq, n_q),
        in_specs=[pl.BlockSpec((None, C_HEADS, HEAD_DIM, MOBA_BLOCK), lambda s, i: (s, 0, 0, i)),
                  pl.BlockSpec((None, C_HEADS, seq_len, HEAD_DIM), lambda s, i: (s, 0, 0, 0)),
                  pl.BlockSpec((None, C_HEADS, HEAD_DIM, seq_len), lambda s, i: (s, 0, 0, 0)),
                  pl.BlockSpec((None, C_HEADS, MOBA_BLOCK, HEAD_DIM), lambda s, i: (s, 0, i, 0)),
                  pl.BlockSpec((None, C_HEADS, HEAD_DIM, MOBA_BLOCK), lambda s, i: (s, 0, 0, i)),
                  pl.BlockSpec((None, n_q, C_WIDTH), lambda s, i: (s, 0, 0)),
                  pl.BlockSpec((MOBA_BLOCK, C_WIDTH), lambda s, i: (s * n_q + i, z_col))],
        out_specs=pl.BlockSpec((MOBA_BLOCK, C_WIDTH), lambda s, i: (s * n_q + i, 0)),
        out_shape=jax.ShapeDtypeStruct((n_seq * seq_len, C_WIDTH), BF16),
        scratch_shapes=[pltpu.VMEM((C_HEADS, 1, MOBA_BLOCK), F32),
                        pltpu.VMEM((C_HEADS, 1, MOBA_BLOCK), F32),
                        pltpu.VMEM((C_HEADS, HEAD_DIM, MOBA_BLOCK), F32)],
        compiler_params=_cparams(("parallel", "arbitrary")),
        name="moba_prompt",
    )(qt, kr, vt, kr, vt, kmean, pc)


def _moba_fused_kernel(pc_ref, cos_ref, sin_ref, qg_ref, kg_ref, gsum_ref,
                       kt_ref, vt_ref, y_ref,
                       kr_scr, vb_scr, kmean_scr, m_scr, l_scr, acc_scr):
    i = pl.program_id(1)
    n_blk = kr_scr.shape[0]
    tq = pc_ref.shape[0]
    scale = HEAD_DIM ** -0.5
    heads = range(C_HEADS)
    hsl = [slice(h * HEAD_DIM, (h + 1) * HEAD_DIM) for h in heads]

    gsum = gsum_ref[...]
    cos = cos_ref[...]
    sin = sin_ref[...]
    qs = _qk_norm_rope(pc_ref[:, 0:C_WIDTH], qg_ref[...], cos, sin, gsum)
    ks = _qk_norm_rope(pc_ref[:, C_WIDTH:2 * C_WIDTH], kg_ref[...], cos, sin, gsum)
    @pl.when(i == 0)
    def _():
        kmean_scr[...] = jnp.zeros(kmean_scr.shape, F32)

    mean_row = jnp.concatenate([jnp.mean(ks[p], axis=0, keepdims=True)
                                for p in range(C_HEADS // 2)], axis=1)
    blk_row = lax.broadcasted_iota(jnp.int32, kmean_scr.shape, 0)
    kmean = jnp.where(blk_row == i, mean_row, kmean_scr[...])
    kmean_scr[...] = kmean

    qt, k_own, v_own = [], [], []
    for p in range(C_HEADS // 2):
        vp = pc_ref[:, 2 * C_WIDTH + p * LANES:2 * C_WIDTH + (p + 1) * LANES]
        qpt = qs[p].T
        kpt = ks[p].T
        vpt = vp.T
        for j in range(2):
            h = 2 * p + j
            ds_ = slice(j * HEAD_DIM, (j + 1) * HEAD_DIM)
            qt.append(qpt[ds_, :])
            kt_ref[h] = kpt[ds_, :]
            vt_ref[h] = vpt[ds_, :]
            k_own.append(ks[p][:, ds_].astype(BF16))
            v_own.append(vpt[ds_, :].astype(BF16))
            kr_scr[i, h] = k_own[h]
            vb_scr[i, h] = v_own[h]

    own_blk = jnp.full((1, tq), i, jnp.int32)
    gate = [_mm3(_split(kmean[:, hsl[h]]), _split(qt[h])) for h in heads]
    bias = []
    for h in heads:
        sel = _topk_past_mask([gate[h][b:b + 1, :] for b in range(n_blk)], own_blk)
        bias.append([jnp.where(sel[b] > 0.5, 0.0, NEG_BIG) for b in range(n_blk - 1)])
    qb = [(qt[h] * (scale * LOG2_E)).astype(BF16) for h in heads]

    key = lax.broadcasted_iota(jnp.int32, (MOBA_BLOCK, tq), 0)
    qry = lax.broadcasted_iota(jnp.int32, (MOBA_BLOCK, tq), 1)
    s = [jnp.where(key <= qry, _dot(k_own[h], qb[h]), NEG_BIG) for h in heads]
    m = [s[h].max(axis=0, keepdims=True) for h in heads]
    p = [jnp.exp2(s[h] - m[h]) for h in heads]
    pv = [_dot(v_own[h], p[h].astype(BF16)) for h in heads]
    for h in heads:
        m_scr[h] = m[h]
        l_scr[h] = p[h].sum(axis=0, keepdims=True)
        acc_scr[h] = pv[h]

    for b in range(n_blk - 1):
        @pl.when(b < i)
        def _(b=b):
            s = [_dot(kr_scr[b, h], qb[h]) + bias[h][b] for h in heads]
            m_old = [m_scr[h] for h in heads]
            m_new = [jnp.maximum(m_old[h], s[h].max(axis=0, keepdims=True)) for h in heads]
            p = [jnp.exp2(s[h] - m_new[h]) for h in heads]
            pv = [_dot(vb_scr[b, h], p[h].astype(BF16)) for h in heads]
            for h in heads:
                alpha = jnp.exp2(m_old[h] - m_new[h])
                m_scr[h] = m_new[h]
                l_scr[h] = l_scr[h] * alpha + p[h].sum(axis=0, keepdims=True)
                acc_scr[h] = acc_scr[h] * alpha + pv[h]

    for pr in range(C_HEADS // 2):
        out_t = jnp.concatenate([acc_scr[2 * pr + j] / l_scr[2 * pr + j] for j in range(2)], axis=0)
        cs = slice(pr * LANES, (pr + 1) * LANES)
        z = pc_ref[:, 3 * C_WIDTH + pr * LANES:3 * C_WIDTH + (pr + 1) * LANES]
        y_ref[:, cs] = (out_t.T * _silu(z)).astype(y_ref.dtype)


def _moba_fused(pc, cos, sin, qg, kg, gsum, n_seq, seq_len):
    tm = MOBA_BLOCK
    n_q = seq_len // tm
    assert n_q * tm == seq_len
    tr = lambda: pl.BlockSpec((None, C_HEADS, HEAD_DIM, tm), lambda s, i: (s, 0, 0, i))
    return pl.pallas_call(
        _moba_fused_kernel,
        grid=(n_seq, n_q),
        in_specs=[pl.BlockSpec((tm, C_COLS), lambda s, i: (s * n_q + i, 0)),
                  pl.BlockSpec((tm, LANES), lambda s, i: (i, 0)),
                  pl.BlockSpec((tm, LANES), lambda s, i: (i, 0)),
                  pl.BlockSpec((1, C_WIDTH), lambda s, i: (0, 0)),
                  pl.BlockSpec((1, C_WIDTH), lambda s, i: (0, 0)),
                  pl.BlockSpec((LANES, LANES), lambda s, i: (0, 0))],
        out_specs=[tr(), tr(),
                   pl.BlockSpec((tm, C_WIDTH), lambda s, i: (s * n_q + i, 0))],
        out_shape=[jax.ShapeDtypeStruct((n_seq, C_HEADS, HEAD_DIM, seq_len), F32),
                   jax.ShapeDtypeStruct((n_seq, C_HEADS, HEAD_DIM, seq_len), F32),
                   jax.ShapeDtypeStruct((n_seq * seq_len, C_WIDTH), BF16)],
        scratch_shapes=[pltpu.VMEM((n_q, C_HEADS, tm, HEAD_DIM), BF16),
                        pltpu.VMEM((n_q, C_HEADS, HEAD_DIM, tm), BF16),
                        pltpu.VMEM((-(-n_q // 8) * 8, C_WIDTH), F32),
                        pltpu.VMEM((C_HEADS, 1, tm), F32),
                        pltpu.VMEM((C_HEADS, 1, tm), F32),
                        pltpu.VMEM((C_HEADS, HEAD_DIM, tm), F32)],
        compiler_params=_cparams(("parallel", "arbitrary")),
        name="moba_prompt",
    )(pc, cos, sin, qg, kg, gsum)


def _moba_sample_kernel(pt_ref, q_ref, kn_ref, pc_ref, *refs, n_pages, n_sub):
    del pt_ref
    n_in = n_sub * n_pages
    k_refs = [refs[u * n_pages:(u + 1) * n_pages] for u in range(n_sub)]
    v_refs = [refs[n_in + u * n_pages:n_in + (u + 1) * n_pages] for u in range(n_sub)]
    y_ref = refs[2 * n_in]
    s_len = q_ref.shape[0] // n_sub
    page = refs[0].shape[1]
    per_blk = MOBA_BLOCK // page
    n_past = n_pages // per_blk
    scale = HEAD_DIM ** -0.5
    hsl = [slice(h * HEAD_DIM, (h + 1) * HEAD_DIM) for h in range(C_HEADS)]
    probs_ = [(u, h) for u in range(n_sub) for h in range(C_HEADS)]
    qrows = lambda u: slice(u * s_len, (u + 1) * s_len)

    q_s = {(u, h): _split(q_ref[qrows(u), hsl[h]]) for (u, h) in probs_}
    s_rows = []
    for (u, h) in probs_:
        parts = [_split(k_refs[u][g][hsl[h], :]) for g in range(n_pages)]
        k_s = (jnp.concatenate([p[0] for p in parts], axis=1),
               jnp.concatenate([p[1] for p in parts], axis=1))
        s_rows.append(_mm3(q_s[(u, h)], k_s))
    s_all = jnp.concatenate(s_rows, axis=0)
    so = jnp.concatenate([_mm3(q_s[(u, h)], _split(kn_ref[qrows(u), hsl[h]]), _NT)
                          for (u, h) in probs_], axis=0)

    blocks = [s_all[:, b * MOBA_BLOCK:(b + 1) * MOBA_BLOCK] for b in range(n_past)]
    gates = [blk.sum(axis=-1, keepdims=True) * (1.0 / MOBA_BLOCK) for blk in blocks]
    sel = _topk_past_mask(gates, n_past)
    n_rows = len(probs_) * s_len
    row = lax.broadcasted_iota(jnp.int32, (n_rows, s_len), 0) % s_len
    col = lax.broadcasted_iota(jnp.int32, (n_rows, s_len), 1)
    so = jnp.where(col <= row, so * scale, NEG_BIG)
    pieces = [jnp.where(sel[b] > 0.5, blocks[b] * scale, NEG_BIG) for b in range(n_past)]
    m = so.max(axis=-1, keepdims=True)
    for b in range(n_past):
        m = jnp.maximum(m, pieces[b].max(axis=-1, keepdims=True))
    po = jnp.exp(so - m)
    den = po.sum(axis=-1, keepdims=True)
    probs = []
    for b in range(n_past):
        pb = jnp.exp(pieces[b] - m)
        den = den + pb.sum(axis=-1, keepdims=True)
        probs.append(pb.astype(BF16))
    p_all = jnp.concatenate(probs, axis=1)
    po = po.astype(BF16)

    for i, (u, h) in enumerate(probs_):
        rs = slice(i * s_len, (i + 1) * s_len)
        v_t = jnp.concatenate([v_refs[u][g][hsl[h], :].astype(BF16) for g in range(n_pages)],
                              axis=1)
        vn = pc_ref[qrows(u), 2 * C_WIDTH + h * HEAD_DIM:2 * C_WIDTH + (h + 1) * HEAD_DIM]
        z = pc_ref[qrows(u), 3 * C_WIDTH + h * HEAD_DIM:3 * C_WIDTH + (h + 1) * HEAD_DIM]
        acc = _dot_nt(p_all[rs], v_t) + _dot(po[rs], vn.astype(BF16))
        y_ref[qrows(u), hsl[h]] = (acc / den[rs] * _silu(z)).astype(y_ref.dtype)


def _moba_sample(page_table, qn, kn, pc, cache_kt, cache_vt, layer, n_seq, s_len, n_sub):
    n_pages = page_table.shape[1]
    page = cache_kt.shape[-1]
    assert MOBA_BLOCK % page == 0 and (n_pages * page) % MOBA_BLOCK == 0 and n_seq % n_sub == 0
    rows = n_sub * s_len
    cache_spec = lambda u, g: pl.BlockSpec(
        (None, None, C_WIDTH, page), lambda s, pt: (layer, pt[s * n_sub + u, g], 0, 0))
    cache_specs = [cache_spec(u, g) for u in range(n_sub) for g in range(n_pages)]
    grid_spec = pltpu.PrefetchScalarGridSpec(
        num_scalar_prefetch=1,
        grid=(n_seq // n_sub,),
        in_specs=[pl.BlockSpec((rows, C_WIDTH), lambda s, pt: (s, 0)),
                  pl.BlockSpec((rows, C_WIDTH), lambda s, pt: (s, 0)),
                  pl.BlockSpec((rows, C_COLS), lambda s, pt: (s, 0))] + cache_specs * 2,
        out_specs=pl.BlockSpec((rows, C_WIDTH), lambda s, pt: (s, 0)))
    n_in = n_sub * n_pages
    return pl.pallas_call(
        functools.partial(_moba_sample_kernel, n_pages=n_pages, n_sub=n_sub),
        grid_spec=grid_spec,
        out_shape=jax.ShapeDtypeStruct((n_seq * s_len, C_WIDTH), BF16),
        compiler_params=_cparams(("parallel",)),
        name="moba_sample",
    )(page_table, qn, kn, pc, *([cache_kt] * n_in), *([cache_vt] * n_in))


def _out_proj_kernel(x_ref, ya_ref, yb_ref, yc_ref, w_ref, o_ref):
    acc = _dot(ya_ref[...], w_ref[0:A_WIDTH, :])
    acc = acc + _dot(yb_ref[...], w_ref[A_WIDTH:A_WIDTH + B_WIDTH, :])
    acc = acc + _dot(yc_ref[...], w_ref[A_WIDTH + B_WIDTH:, :])
    o_ref[...] = x_ref[...] + acc


def _out_proj(x2d, ya, yb, yc, w_bf16, tm):
    t, d = x2d.shape
    assert t % tm == 0
    row = lambda w: pl.BlockSpec((tm, w), lambda i: (i, 0))
    return pl.pallas_call(
        _out_proj_kernel,
        grid=(t // tm,),
        in_specs=[row(d), row(A_WIDTH), row(B_WIDTH), row(C_WIDTH),
                  pl.BlockSpec(w_bf16.shape, lambda i: (0, 0))],
        out_specs=row(d),
        out_shape=jax.ShapeDtypeStruct((t, d), F32),
        compiler_params=_cparams(("parallel",)),
        name="out_proj",
    )(x2d, ya, yb, yc, w_bf16)


def _group_sum_matrix(width):
    g = jnp.arange(width) // HEAD_DIM
    return (g[:, None] == g[None, :]).astype(BF16)


def _rope_tables(pos):
    half = HEAD_DIM // 2
    inv = ROPE_THETA ** (-jnp.arange(half, dtype=F32) / half)
    ang = pos.astype(F32)[:, None] * inv[None, :]
    cos = jnp.cos(ang)
    sin = jnp.sin(ang)
    return jnp.tile(cos, (1, 4)), jnp.tile(jnp.concatenate([-sin, sin], axis=1), (1, 2))


def _tile_heads(g, n_heads):
    return jnp.tile(g.reshape(1, HEAD_DIM), (1, n_heads))


def kernel(x_prompt, x_sample, cache_k, cache_v, state_wkv, state_shift, page_table, norm_g, w_in, w_out, a_ln_g, a_ln_b, a_ws, a_bs, b_mu, b_w0, b_w2, b_a0, b_a2, b_kk, b_ka, b_rk, b_lnx_g, b_lnx_b, c_qn_g, c_kn_g):
    depth = w_in.shape[0]
    n_p, seq_len, d_model = x_prompt.shape
    n_s, s_len, _ = x_sample.shape
    page = cache_k.shape[2]
    past_len = page_table.shape[1] * page
    assert seq_len % MOBA_BLOCK == 0 and seq_len % A_CHUNK == 0 and seq_len % RWKV_CHUNK == 0
    assert past_len % MOBA_BLOCK == 0 and s_len % 8 == 0 and s_len <= min(A_CHUNK, RWKV_CHUNK)
    n_blk = seq_len // MOBA_BLOCK

    gsum_a = gsum_b = _group_sum_matrix(LANES)
    cos_p, sin_p = _rope_tables(jnp.arange(seq_len))
    cos_s, sin_s = _rope_tables(past_len + jnp.arange(n_s * s_len) % s_len)
    cache_kt = jnp.transpose(cache_k, (0, 1, 3, 4, 2)).reshape(depth, -1, C_WIDTH, page)
    cache_vt = jnp.transpose(cache_v, (0, 1, 3, 4, 2)).reshape(depth, -1, C_WIDTH, page)
    w_in_b = w_in.astype(BF16)
    w_out_b = w_out.astype(BF16)
    zero_shift = jnp.zeros((n_p, 1, B_SHIFT_W), F32)
    zero_wkv = jnp.zeros((n_p, B_HEADS, HEAD_DIM, HEAD_DIM), F32)
    seqs_per_tile = A_CHUNK // s_len
    eye_tile = jnp.eye(seqs_per_tile, dtype=F32)

    hp = x_prompt.reshape(n_p * seq_len, d_model)
    hs = x_sample.reshape(n_s * s_len, d_model)
    outs = {k: [] for k in ("kp", "vp", "ks", "vs", "wkvp", "wkvs", "shp", "shs", "gvs")}
    for l in range(depth):
        g = norm_g[l].reshape(1, d_model)
        lng_a = a_ln_g[l].reshape(1, A_WIDTH)
        lnb_a = a_ln_b[l].reshape(1, A_WIDTH)
        bs_full = jnp.repeat(a_bs[l].T, HEAD_DIM, axis=1)
        ws_s = jnp.stack([jnp.kron(eye_tile, a_ws[l, h, :s_len, :s_len]) for h in range(A_HEADS)])
        bs_s = jnp.tile(bs_full[:s_len], (seqs_per_tile, 1))
        prm = dict(mu=b_mu[l].reshape(1, -1), w0=b_w0[l].reshape(1, -1), w2=b_w2[l],
                   a0=b_a0[l].reshape(1, -1), a2=b_a2[l], kk=b_kk[l].reshape(1, -1),
                   ka=b_ka[l].reshape(1, -1), rk=b_rk[l].reshape(1, -1),
                   lnx_g=b_lnx_g[l].reshape(1, -1), lnx_b=b_lnx_b[l].reshape(1, -1))
        qg = _tile_heads(c_qn_g[l], C_HEADS)
        kg = _tile_heads(c_kn_g[l], C_HEADS)

        pa, pb, pc = _in_proj(hp, g, w_in_b[l], min(512, hp.shape[0]))
        ya, _ = _gmlp(pa, a_ws[l], bs_full, lng_a, lnb_a, gsum_a, 4 * A_CHUNK, 4)
        yb, wkvp, shp = _rwkv(pb, zero_shift, zero_wkv, prm, gsum_b, n_p, 1,
                              min(RWKV_ROWS, seq_len), RWKV_CHUNK)
        kt, vt, yc = _moba_fused(pc, cos_p, sin_p, qg, kg, gsum_b, n_p, seq_len)
        hp = _out_proj(hp, ya, yb, yc, w_out_b[l], min(512, hp.shape[0]))
        outs["kp"].append(jnp.transpose(kt, (0, 3, 1, 2)))
        outs["vp"].append(jnp.transpose(vt, (0, 3, 1, 2)))
        outs["wkvp"].append(wkvp)
        outs["shp"].append(shp.reshape(n_p, B_SHIFT_W))

        pa, pb, pc = _in_proj(hs, g, w_in_b[l], min(512, hs.shape[0]))
        ya, gv = _gmlp(pa, ws_s, bs_s, lng_a, lnb_a, gsum_a, A_CHUNK, 1)
        yb, wkvs, shs = _rwkv(pb, state_shift[l].reshape(n_s, 1, B_SHIFT_W), state_wkv[l],
                              prm, gsum_b, n_s, math.gcd(RWKV_SAMPLE_SEQS, n_s), s_len, s_len)
        qn, kn = _prep_sample(pc, cos_s, sin_s, qg, kg, gsum_b, min(256, hs.shape[0]))
        yc = _moba_sample(page_table, qn, kn, pc, cache_kt, cache_vt, l, n_s, s_len,
                          math.gcd(MOBA_SAMPLE_SEQS, n_s))
        hs = _out_proj(hs, ya, yb, yc, w_out_b[l], min(512, hs.shape[0]))
        outs["ks"].append(kn.reshape(n_s, s_len, C_HEADS, HEAD_DIM))
        outs["vs"].append(pc[:, 2 * C_WIDTH:3 * C_WIDTH].reshape(n_s, s_len, C_HEADS, HEAD_DIM))
        outs["wkvs"].append(wkvs)
        outs["shs"].append(shs.reshape(n_s, B_SHIFT_W))
        outs["gvs"].append(gv.reshape(n_s, s_len, A_WIDTH))

    st = {k: jnp.stack(v) for k, v in outs.items()}
    return (hp.reshape(n_p, seq_len, d_model), hs.reshape(n_s, s_len, d_model),
            st["kp"], st["vp"], st["ks"], st["vs"], st["wkvp"], st["wkvs"],
            st["shp"], st["shs"], st["gvs"])
```

```python
import functools
import math

import jax
import jax.numpy as jnp
from jax import lax
from jax.experimental import pallas as pl
from jax.experimental.pallas import tpu as pltpu

F32 = jnp.float32
BF16 = jnp.bfloat16

LANES = 128
HEAD_DIM = 64
A_HEADS = 4
A_WIDTH = A_HEADS * HEAD_DIM
A_CHUNK = 128
B_HEADS = 6
B_WIDTH = B_HEADS * HEAD_DIM
B_LORA = 64
B_SHIFT_W = 3 * B_WIDTH + 2 * B_LORA
C_HEADS = 6
C_WIDTH = C_HEADS * HEAD_DIM
MOBA_BLOCK = 256
MOBA_TOPK = 3
ROPE_THETA = 10000.0
A_COLS = 3 * A_WIDTH
B_COLS = B_SHIFT_W + B_WIDTH
C_COLS = 4 * C_WIDTH
RMS_EPS = 1e-6
LN_EPS = 1e-5
GN_EPS = 64e-5
NEG_BIG = -1e30
LOG2_E = 1.4426950408889634

VMEM_LIMIT_BYTES = 56 * 1024 * 1024
RWKV_CHUNK = 64
RWKV_ROWS = 256
PROJ_ROWS = 512
IN_PROJ_SUB = 128
GMLP_CHUNKS = 4
PREP_ROWS = 256
MOBA_SAMPLE_SEQS = 2
RWKV_PROMPT_SEQS = 4
RWKV_SAMPLE_SEQS = 8


def _cparams(sem):
    return pltpu.CompilerParams(dimension_semantics=sem,
                                vmem_limit_bytes=VMEM_LIMIT_BYTES)


def _dot(a, b):
    return jnp.dot(a, b, preferred_element_type=F32)


def _dot_nt(a, b):
    return lax.dot_general(a, b, (((1,), (1,)), ((), ())), preferred_element_type=F32)


def _silu(z):
    return z * jax.nn.sigmoid(z)


def _split(x):
    hi = x.astype(BF16)
    lo = (x - hi.astype(F32)).astype(BF16)
    return hi, lo


def _cat_rows(a, b):
    return jnp.concatenate([a[0], b[0]], axis=0), jnp.concatenate([a[1], b[1]], axis=0)


_NN = (((1,), (0,)), ((), ()))
_NT = (((1,), (1,)), ((), ()))
_TN = (((0,), (0,)), ((), ()))


def _mm3(a, b, dims=_NN):
    d = lambda x, y: lax.dot_general(x, y, dims, preferred_element_type=F32)
    return d(a[0], b[0]) + d(a[0], b[1]) + d(a[1], b[0])


def _mm_exact_rhs(x, g_bf16, passes=2):
    hi = x.astype(BF16)
    lo = (x - hi.astype(F32)).astype(BF16) if passes == 2 else None
    cols = []
    for p in range(x.shape[1] // LANES):
        cs = slice(p * LANES, (p + 1) * LANES)
        acc = _dot(hi[:, cs], g_bf16)
        if passes == 2:
            acc = acc + _dot(lo[:, cs], g_bf16)
        cols.append(acc)
    return jnp.concatenate(cols, axis=1)


def _in_proj_kernel(x_ref, g_ref, w_ref, pa_ref, pb_ref, pc_ref, *, n_sub):
    sub = x_ref.shape[0] // n_sub
    for j in range(n_sub):
        rs = slice(j * sub, (j + 1) * sub)
        x = x_ref[rs, :]
        ms = jnp.mean(x * x, axis=-1, keepdims=True)
        h = (x * lax.rsqrt(ms + RMS_EPS) * g_ref[...]).astype(BF16)
        p = _dot(h, w_ref[...])
        pa_ref[rs, :] = p[:, :A_COLS]
        pb_ref[rs, :] = p[:, A_COLS:A_COLS + B_COLS]
        pc_ref[rs, :] = p[:, A_COLS + B_COLS:]


def _in_proj(x2d, g, w_bf16, tm):
    t, d = x2d.shape
    n = w_bf16.shape[1]
    assert t % tm == 0 and tm % IN_PROJ_SUB == 0
    return pl.pallas_call(
        functools.partial(_in_proj_kernel, n_sub=tm // IN_PROJ_SUB),
        grid=(t // tm,),
        in_specs=[pl.BlockSpec((tm, d), lambda i: (i, 0)),
                  pl.BlockSpec((1, d), lambda i: (0, 0)),
                  pl.BlockSpec((d, n), lambda i: (0, 0))],
        out_specs=[pl.BlockSpec((tm, A_COLS), lambda i: (i, 0)),
                   pl.BlockSpec((tm, B_COLS), lambda i: (i, 0)),
                   pl.BlockSpec((tm, C_COLS), lambda i: (i, 0))],
        out_shape=[jax.ShapeDtypeStruct((t, A_COLS), F32),
                   jax.ShapeDtypeStruct((t, B_COLS), F32),
                   jax.ShapeDtypeStruct((t, C_COLS), F32)],
        compiler_params=_cparams(("parallel",)),
        name="in_proj",
    )(x2d, g, w_bf16)


def _gmlp_kernel(pa_ref, ws_ref, bs_ref, lng_ref, lnb_ref, gsum_ref, ya_ref, vn_ref,
                 *, n_sub):
    rows = pa_ref.shape[0]
    sub = rows // n_sub
    u = pa_ref[:, 0:A_WIDTH]
    v = pa_ref[:, A_WIDTH:2 * A_WIDTH]
    z = pa_ref[:, 2 * A_WIDTH:3 * A_WIDTH]
    gsum = gsum_ref[...]
    mu = _mm_exact_rhs(v, gsum) * (1.0 / HEAD_DIM)
    d = v - mu
    var = _mm_exact_rhs(d * d, gsum) * (1.0 / HEAD_DIM)
    vn = d * lax.rsqrt(var + LN_EPS) * lng_ref[...] + lnb_ref[...]
    vn_ref[...] = vn
    ri = lax.broadcasted_iota(jnp.int32, (sub, sub), 0)
    ci = lax.broadcasted_iota(jnp.int32, (sub, sub), 1)
    causal = ri >= ci
    vn_b = vn.astype(BF16)
    for h in range(A_HEADS):
        wm = jnp.where(causal, ws_ref[h], 0.0).astype(BF16)
        sl = slice(h * HEAD_DIM, (h + 1) * HEAD_DIM)
        for c in range(n_sub):
            rs = slice(c * sub, (c + 1) * sub)
            mixed = _dot(wm, vn_b[rs, sl]) + bs_ref[:, sl]
            ya_ref[rs, sl] = (u[rs, sl] * mixed * _silu(z[rs, sl])).astype(ya_ref.dtype)


def _gmlp(pa, ws_eff, bs_eff, lng, lnb, gsum, rows, n_sub):
    t = pa.shape[0]
    sub = rows // n_sub
    assert t % rows == 0 and ws_eff.shape == (A_HEADS, sub, sub)
    return pl.pallas_call(
        functools.partial(_gmlp_kernel, n_sub=n_sub),
        grid=(t // rows,),
        in_specs=[pl.BlockSpec((rows, A_COLS), lambda i: (i, 0)),
                  pl.BlockSpec((A_HEADS, sub, sub), lambda i: (0, 0, 0)),
                  pl.BlockSpec((sub, A_WIDTH), lambda i: (0, 0)),
                  pl.BlockSpec((1, A_WIDTH), lambda i: (0, 0)),
                  pl.BlockSpec((1, A_WIDTH), lambda i: (0, 0)),
                  pl.BlockSpec((LANES, LANES), lambda i: (0, 0))],
        out_specs=[pl.BlockSpec((rows, A_WIDTH), lambda i: (i, 0)),
                   pl.BlockSpec((rows, A_WIDTH), lambda i: (i, 0))],
        out_shape=[jax.ShapeDtypeStruct((t, A_WIDTH), BF16),
                   jax.ShapeDtypeStruct((t, A_WIDTH), F32)],
        compiler_params=_cparams(("parallel",)),
        name="gmlp",
    )(pa, ws_eff, bs_eff, lng, lnb, gsum)


def _rwkv_kernel(pb_ref, shift0_ref, wkv0_ref, mu_ref, w0_ref, w2_ref, a0_ref, a2_ref,
                 kk_ref, ka_ref, rk_ref, lng_ref, lnb_ref, gsum_ref,
                 yb_ref, wkv_ref, shift_ref, state_scr, prev_scr, o_scr, *, chunk, n_sub):
    c = pl.program_id(1)
    n_c = pl.num_programs(1)
    sub_rows = pb_ref.shape[1]
    rows = n_sub * sub_rows
    n_ch = sub_rows // chunk

    @pl.when(c == 0)
    def _():
        state_scr[...] = wkv0_ref[...]
        prev_scr[...] = shift0_ref[...]

    sb = jnp.concatenate([pb_ref[q, :, 0:B_SHIFT_W] for q in range(n_sub)], axis=0)
    zb = jnp.concatenate([pb_ref[q, :, B_SHIFT_W:B_COLS] for q in range(n_sub)], axis=0)
    row_id = lax.broadcasted_iota(jnp.int32, (rows, 1), 0)
    prev = pltpu.roll(sb, 1, axis=0)
    last_rows = []
    for q in range(n_sub):
        prev = jnp.where(row_id == q * sub_rows, prev_scr[q], prev)
        last_rows.append(sb[(q + 1) * sub_rows - 1:(q + 1) * sub_rows, :])
        prev_scr[q] = last_rows[q]
    xs = sb + (prev - sb) * mu_ref[...]
    r = xs[:, 0:B_WIDTH]
    k = xs[:, B_WIDTH:2 * B_WIDTH]
    v = xs[:, 2 * B_WIDTH:3 * B_WIDTH]
    wd = xs[:, 3 * B_WIDTH:3 * B_WIDTH + B_LORA]
    ad = xs[:, 3 * B_WIDTH + B_LORA:B_SHIFT_W]

    gsum = gsum_ref[...]
    y = -(w0_ref[...] + _dot(jnp.tanh(wd).astype(BF16), w2_ref[...].astype(BF16)))
    softplus = jnp.maximum(y, 0.0) + jnp.log(1.0 + jnp.exp(-jnp.abs(y)))
    logdecay = -jnp.exp(-softplus - 0.5)
    a = jax.nn.sigmoid(a0_ref[...] + _dot(ad.astype(BF16), a2_ref[...].astype(BF16)))
    kk = k * kk_ref[...]
    kk = kk * lax.rsqrt(jnp.maximum(_mm_exact_rhs(kk * kk, gsum, 1), 1e-24))
    k2 = k * (1.0 + (a - 1.0) * ka_ref[...])

    ri = lax.broadcasted_iota(jnp.int32, (chunk, chunk), 0)
    ci = lax.broadcasted_iota(jnp.int32, (chunk, chunk), 1)
    tri = jnp.where(ri >= ci, 1.0, 0.0).astype(BF16)
    cums, tots = [], []
    for j in range(rows // chunk):
        ld = logdecay[j * chunk:(j + 1) * chunk]
        l1 = ld.astype(BF16)
        rem = ld - l1.astype(F32)
        l2 = rem.astype(BF16)
        l3 = (rem - l2.astype(F32)).astype(BF16)
        cj = _dot(tri, l1) + _dot(tri, l2) + _dot(tri, l3)
        cums.append(cj)
        tots.append(jnp.broadcast_to(cj[chunk - 1:chunk, :], cj.shape))
    cum = jnp.concatenate(cums, axis=0)
    tot = jnp.concatenate(tots, axis=0)
    e_neg = jnp.exp(-cum)
    tail = jnp.exp(tot - cum)
    g_tot = jnp.exp(tot)
    kka = kk * a
    at = (-kk * jnp.exp(cum - logdecay)).astype(BF16)
    rt = (r * jnp.exp(cum)).astype(BF16)
    bp = (kka * e_neg).astype(BF16)
    kp = (k2 * e_neg).astype(BF16)
    bh = (kka * tail).astype(BF16)
    kh = (k2 * tail).astype(BF16)
    vb = v.astype(BF16)

    li = lax.broadcasted_iota(jnp.int32, (2 * chunk, 2 * chunk), 0)
    lj = lax.broadcasted_iota(jnp.int32, (2 * chunk, 2 * chunk), 1) % chunk
    lower4 = jnp.where(li < chunk, li - 1, li - chunk) >= lj
    xi = lax.broadcasted_iota(jnp.int32, (chunk, 2 * chunk), 0)
    xj = lax.broadcasted_iota(jnp.int32, (chunk, 2 * chunk), 1)
    left = xj < chunk
    eye_r = (xj - chunk == xi).astype(F32)
    n_lvl = max(1, int(math.ceil(math.log2(chunk))))
    mm = lambda x, y, dims=_NN: lax.dot_general(x, y, dims, preferred_element_type=F32)
    cat = lambda x, y: jnp.concatenate([x, y], axis=0)

    hsl = [slice(h * HEAD_DIM, (h + 1) * HEAD_DIM) for h in range(B_HEADS)]
    chains = [(h, q) for q in range(n_sub) for h in range(B_HEADS)]
    rsl = lambda t: slice(t[1] * sub_rows + t[2] * chunk, t[1] * sub_rows + (t[2] + 1) * chunk)
    cut = lambda x, t: x[rsl(t), hsl[t[0]]]
    zero_v = jnp.zeros((chunk, HEAD_DIM), BF16)
    v_h, y_m, wr, pm_b, bk, qv = {}, {}, {}, {}, {}, {}
    state = {(h, q): state_scr[q, h] for (h, q) in chains}

    def independent(chs):
        trips = [cq + (ch,) for ch in chs for cq in chains]
        at_h = {t: cut(at, t) for t in trips}
        rt_h = {t: cut(rt, t) for t in trips}
        for t in trips:
            v_h[t] = cut(vb, t)
        g = {t: jnp.where(lower4, mm(cat(at_h[t], rt_h[t]), cat(cut(bp, t), cut(kp, t)), _NT), 0.0)
             for t in trips}
        yield
        bq_v = {t: mm(g[t].astype(BF16), cat(zero_v, v_h[t])) for t in trips}
        yield
        x = {t: jnp.where(left, g[t][:chunk], eye_r) for t in trips}
        for _ in range(n_lvl):
            res = {t: mm(x[t][:, :chunk].astype(BF16), x[t].astype(BF16)) for t in trips}
            x = {t: res[t] + jnp.where(left, 0.0, x[t]) for t in trips}
            yield
        wy = {t: mm(x[t][:, chunk:].astype(BF16),
                    jnp.concatenate([at_h[t].astype(F32), bq_v[t][:chunk]], axis=1).astype(BF16))
              for t in trips}
        for t in trips:
            y_m[t] = wy[t][:, HEAD_DIM:]
            wr[t] = cat(wy[t][:, :HEAD_DIM].astype(BF16), rt_h[t])
            pm_b[t] = g[t][chunk:, :chunk].astype(BF16)
            bk[t] = cat(cut(bh, t), cut(kh, t))
            qv[t] = bq_v[t][chunk:]
        yield

    def recurrent(ch):
        ws = {cq: mm(wr[cq + (ch,)], state[cq].astype(BF16), _NT) for cq in chains}
        yield
        u_b = {cq: (ws[cq][:chunk] + y_m[cq + (ch,)]).astype(BF16) for cq in chains}
        for cq in chains:
            t = cq + (ch,)
            r0 = rsl(t).start
            state[cq] = (state[cq] * g_tot[r0:r0 + 1, hsl[cq[0]]]
                         + mm(cat(u_b[cq], v_h[t]), bk[t], _TN))
        yield
        for cq in chains:
            t = cq + (ch,)
            o_scr[rsl(t), hsl[cq[0]]] = ws[cq][chunk:] + mm(pm_b[t], u_b[cq]) + qv[t]
        yield

    for _ in independent(range(n_ch)):
        pass
    for ch in range(n_ch):
        for _ in recurrent(ch):
            pass
    for (h, q) in chains:
        state_scr[q, h] = state[(h, q)]

    o = o_scr[...]
    m = _mm_exact_rhs(o, gsum, 1) * (1.0 / HEAD_DIM)
    d = o - m
    var = _mm_exact_rhs(d * d, gsum, 1) * (1.0 / HEAD_DIM)
    on = d * lax.rsqrt(var + GN_EPS) * lng_ref[...] + lnb_ref[...]
    bonus = _mm_exact_rhs(r * k2 * rk_ref[...], gsum, 1) * v
    yb = ((on + bonus) * _silu(zb)).astype(yb_ref.dtype)
    for q in range(n_sub):
        yb_ref[q] = yb[q * sub_rows:(q + 1) * sub_rows]

    @pl.when(c == n_c - 1)
    def _():
        wkv_ref[...] = state_scr[...]
        for q in range(n_sub):
            shift_ref[q] = last_rows[q]


def _rwkv(pb, shift0, wkv0, prm, gsum, n_seq, n_sub, sub_rows, chunk):
    t = pb.shape[0]
    seq_rows = t // n_seq
    rows = n_sub * sub_rows
    n_c = seq_rows // sub_rows
    assert n_c * n_seq * sub_rows == t and sub_rows % chunk == 0 and n_seq % n_sub == 0
    n_g = n_seq // n_sub
    vec = lambda w: pl.BlockSpec((1, w), lambda s, c: (0, 0))
    yb, wkv, shift = pl.pallas_call(
        functools.partial(_rwkv_kernel, chunk=chunk, n_sub=n_sub),
        grid=(n_g, n_c),
        in_specs=[pl.BlockSpec((n_sub, sub_rows, B_COLS), lambda s, c: (s, c, 0)),
                  pl.BlockSpec((n_sub, 1, B_SHIFT_W), lambda s, c: (s, 0, 0)),
                  pl.BlockSpec((n_sub, B_HEADS, HEAD_DIM, HEAD_DIM), lambda s, c: (s, 0, 0, 0)),
                  vec(B_SHIFT_W), vec(B_WIDTH),
                  pl.BlockSpec((B_LORA, B_WIDTH), lambda s, c: (0, 0)),
                  vec(B_WIDTH),
                  pl.BlockSpec((B_LORA, B_WIDTH), lambda s, c: (0, 0)),
                  vec(B_WIDTH), vec(B_WIDTH), vec(B_WIDTH), vec(B_WIDTH), vec(B_WIDTH),
                  pl.BlockSpec((LANES, LANES), lambda s, c: (0, 0))],
        out_specs=[pl.BlockSpec((n_sub, sub_rows, B_WIDTH), lambda s, c: (s, c, 0)),
                   pl.BlockSpec((n_sub, B_HEADS, HEAD_DIM, HEAD_DIM), lambda s, c: (s, 0, 0, 0)),
                   pl.BlockSpec((n_sub, 1, B_SHIFT_W), lambda s, c: (s, 0, 0))],
        out_shape=[jax.ShapeDtypeStruct((n_seq, seq_rows, B_WIDTH), BF16),
                   jax.ShapeDtypeStruct((n_seq, B_HEADS, HEAD_DIM, HEAD_DIM), F32),
                   jax.ShapeDtypeStruct((n_seq, 1, B_SHIFT_W), F32)],
        scratch_shapes=[pltpu.VMEM((n_sub, B_HEADS, HEAD_DIM, HEAD_DIM), F32),
                        pltpu.VMEM((n_sub, 1, B_SHIFT_W), F32),
                        pltpu.VMEM((rows, B_WIDTH), F32)],
        compiler_params=_cparams(("parallel", "arbitrary")),
        name="rwkv",
    )(pb.reshape(n_seq, seq_rows, B_COLS), shift0, wkv0, prm["mu"], prm["w0"], prm["w2"],
      prm["a0"], prm["a2"], prm["kk"], prm["ka"], prm["rk"], prm["lnx_g"], prm["lnx_b"], gsum)
    return yb.reshape(t, B_WIDTH), wkv, shift


def _qk_norm_rope(x, g, cos, sin_signed, gsum):
    ms = _mm_exact_rhs(x * x, gsum) * (1.0 / HEAD_DIM)
    y = x * lax.rsqrt(ms + RMS_EPS) * g
    half = HEAD_DIM // 2
    lane = lax.broadcasted_iota(jnp.int32, (1, LANES), 1)
    first = (lane % HEAD_DIM) < half
    outs = []
    for p in range(x.shape[1] // LANES):
        yp = y[:, p * LANES:(p + 1) * LANES]
        partner = jnp.where(first, pltpu.roll(yp, LANES - half, axis=1),
                            pltpu.roll(yp, half, axis=1))
        outs.append(yp * cos + partner * sin_signed)
    return outs


def _prep_sample_kernel(pc_ref, cos_ref, sin_ref, qg_ref, kg_ref, gsum_ref, q_ref, k_ref):
    gsum = gsum_ref[...]
    cos = cos_ref[...]
    sin = sin_ref[...]
    qs = _qk_norm_rope(pc_ref[:, 0:C_WIDTH], qg_ref[...], cos, sin, gsum)
    ks = _qk_norm_rope(pc_ref[:, C_WIDTH:2 * C_WIDTH], kg_ref[...], cos, sin, gsum)
    for p in range(C_HEADS // 2):
        q_ref[:, p * LANES:(p + 1) * LANES] = qs[p]
        k_ref[:, p * LANES:(p + 1) * LANES] = ks[p]


def _prep_sample(pc, cos, sin, qg, kg, gsum, tm):
    t = pc.shape[0]
    assert t % tm == 0
    return pl.pallas_call(
        _prep_sample_kernel,
        grid=(t // tm,),
        in_specs=[pl.BlockSpec((tm, C_COLS), lambda i: (i, 0)),
                  pl.BlockSpec((tm, LANES), lambda i: (i, 0)),
                  pl.BlockSpec((tm, LANES), lambda i: (i, 0)),
                  pl.BlockSpec((1, C_WIDTH), lambda i: (0, 0)),
                  pl.BlockSpec((1, C_WIDTH), lambda i: (0, 0)),
                  pl.BlockSpec((LANES, LANES), lambda i: (0, 0))],
        out_specs=[pl.BlockSpec((tm, C_WIDTH), lambda i: (i, 0)),
                   pl.BlockSpec((tm, C_WIDTH), lambda i: (i, 0))],
        out_shape=[jax.ShapeDtypeStruct((t, C_WIDTH), F32),
                   jax.ShapeDtypeStruct((t, C_WIDTH), F32)],
        compiler_params=_cparams(("parallel",)),
        name="moba_prep_sample",
    )(pc, cos, sin, qg, kg, gsum)


def _topk_past_mask(gates, n_valid):
    n = len(gates)
    valid = [jnp.where(j < n_valid, 1.0, 0.0).astype(F32) for j in range(n)]
    sel = []
    for j in range(n):
        rank = jnp.zeros(gates[j].shape, F32)
        for j2 in range(n):
            if j2 == j:
                continue
            beats = (gates[j2] > gates[j]) if j2 > j else (gates[j2] >= gates[j])
            rank = rank + jnp.where(beats, valid[j2], 0.0)
        sel.append(jnp.where(rank < MOBA_TOPK, valid[j], 0.0))
    return sel


def _moba_fused_kernel(pc_ref, cos_ref, sin_ref, qg_ref, kg_ref, gsum_ref,
                       kt_ref, vt_ref, y_ref,
                       kr_scr, vb_scr, kmean_scr, m_scr, l_scr, acc_scr):
    i = pl.program_id(1)
    n_blk = kr_scr.shape[0]
    tq = pc_ref.shape[0]
    scale = HEAD_DIM ** -0.5
    heads = range(C_HEADS)
    hsl = [slice(h * HEAD_DIM, (h + 1) * HEAD_DIM) for h in heads]

    gsum = gsum_ref[...]
    cos = cos_ref[...]
    sin = sin_ref[...]
    qs = _qk_norm_rope(pc_ref[:, 0:C_WIDTH], qg_ref[...], cos, sin, gsum)
    ks = _qk_norm_rope(pc_ref[:, C_WIDTH:2 * C_WIDTH], kg_ref[...], cos, sin, gsum)
    @pl.when(i == 0)
    def _():
        kmean_scr[...] = jnp.zeros(kmean_scr.shape, F32)

    mean_row = jnp.concatenate([jnp.mean(ks[p], axis=0, keepdims=True)
                                for p in range(C_HEADS // 2)], axis=1)
    blk_row = lax.broadcasted_iota(jnp.int32, kmean_scr.shape, 0)
    kmean = jnp.where(blk_row == i, mean_row, kmean_scr[...])
    kmean_scr[...] = kmean

    qt, k_own, v_own = [], [], []
    for p in range(C_HEADS // 2):
        vp = pc_ref[:, 2 * C_WIDTH + p * LANES:2 * C_WIDTH + (p + 1) * LANES]
        qpt = qs[p].T
        kpt = ks[p].T
        vpt = vp.T
        for j in range(2):
            h = 2 * p + j
            ds_ = slice(j * HEAD_DIM, (j + 1) * HEAD_DIM)
            qt.append(qpt[ds_, :])
            kt_ref[h] = kpt[ds_, :]
            vt_ref[h] = vpt[ds_, :]
            k_own.append(ks[p][:, ds_].astype(BF16))
            v_own.append(vpt[ds_, :].astype(BF16))
            kr_scr[i, h] = k_own[h]
            vb_scr[i, h] = v_own[h]

    own_blk = jnp.full((1, tq), i, jnp.int32)
    gate = [_mm3(_split(kmean[:, hsl[h]]), _split(qt[h])) for h in heads]
    bias = []
    for h in heads:
        sel = _topk_past_mask([gate[h][b:b + 1, :] for b in range(n_blk)], own_blk)
        bias.append([jnp.where(sel[b] > 0.5, 0.0, NEG_BIG) for b in range(n_blk - 1)])
    qb = [(qt[h] * (scale * LOG2_E)).astype(BF16) for h in heads]

    key = lax.broadcasted_iota(jnp.int32, (MOBA_BLOCK, tq), 0)
    qry = lax.broadcasted_iota(jnp.int32, (MOBA_BLOCK, tq), 1)
    s = [jnp.where(key <= qry, _dot(k_own[h], qb[h]), NEG_BIG) for h in heads]
    m = [s[h].max(axis=0, keepdims=True) for h in heads]
    p = [jnp.exp2(s[h] - m[h]) for h in heads]
    pv = [_dot(v_own[h], p[h].astype(BF16)) for h in heads]
    for h in heads:
        m_scr[h] = m[h]
        l_scr[h] = p[h].sum(axis=0, keepdims=True)
        acc_scr[h] = pv[h]

    for b in range(n_blk - 1):
        @pl.when(b < i)
        def _(b=b):
            s = [_dot(kr_scr[b, h], qb[h]) + bias[h][b] for h in heads]
            m_old = [m_scr[h] for h in heads]
            m_new = [jnp.maximum(m_old[h], s[h].max(axis=0, keepdims=True)) for h in heads]
            p = [jnp.exp2(s[h] - m_new[h]) for h in heads]
            pv = [_dot(vb_scr[b, h], p[h].astype(BF16)) for h in heads]
            for h in heads:
                alpha = jnp.exp2(m_old[h] - m_new[h])
                m_scr[h] = m_new[h]
                l_scr[h] = l_scr[h] * alpha + p[h].sum(axis=0, keepdims=True)
                acc_scr[h] = acc_scr[h] * alpha + pv[h]

    for pr in range(C_HEADS // 2):
        out_t = jnp.concatenate([acc_scr[2 * pr + j] / l_scr[2 * pr + j] for j in range(2)], axis=0)
        cs = slice(pr * LANES, (pr + 1) * LANES)
        z = pc_ref[:, 3 * C_WIDTH + pr * LANES:3 * C_WIDTH + (pr + 1) * LANES]
        y_ref[:, cs] = (out_t.T * _silu(z)).astype(y_ref.dtype)


def _moba_fused(pc, cos, sin, qg, kg, gsum, n_seq, seq_len):
    tm = MOBA_BLOCK
    n_q = seq_len // tm
    assert n_q * tm == seq_len
    tr = lambda: pl.BlockSpec((None, C_HEADS, HEAD_DIM, tm), lambda s, i: (s, 0, 0, i))
    return pl.pallas_call(
        _moba_fused_kernel,
        grid=(n_seq, n_q),
        in_specs=[pl.BlockSpec((tm, C_COLS), lambda s, i: (s * n_q + i, 0)),
                  pl.BlockSpec((tm, LANES), lambda s, i: (i, 0)),
                  pl.BlockSpec((tm, LANES), lambda s, i: (i, 0)),
                  pl.BlockSpec((1, C_WIDTH), lambda s, i: (0, 0)),
                  pl.BlockSpec((1, C_WIDTH), lambda s, i: (0, 0)),
                  pl.BlockSpec((LANES, LANES), lambda s, i: (0, 0))],
        out_specs=[tr(), tr(),
                   pl.BlockSpec((tm, C_WIDTH), lambda s, i: (s * n_q + i, 0))],
        out_shape=[jax.ShapeDtypeStruct((n_seq, C_HEADS, HEAD_DIM, seq_len), F32),
                   jax.ShapeDtypeStruct((n_seq, C_HEADS, HEAD_DIM, seq_len), F32),
                   jax.ShapeDtypeStruct((n_seq * seq_len, C_WIDTH), BF16)],
        scratch_shapes=[pltpu.VMEM((n_q, C_HEADS, tm, HEAD_DIM), BF16),
                        pltpu.VMEM((n_q, C_HEADS, HEAD_DIM, tm), BF16),
                        pltpu.VMEM((-(-n_q // 8) * 8, C_WIDTH), F32),
                        pltpu.VMEM((C_HEADS, 1, tm), F32),
                        pltpu.VMEM((C_HEADS, 1, tm), F32),
                        pltpu.VMEM((C_HEADS, HEAD_DIM, tm), F32)],
        compiler_params=_cparams(("parallel", "arbitrary")),
        name="moba_prompt",
    )(pc, cos, sin, qg, kg, gsum)


def _moba_sample_kernel(pt_ref, q_ref, kn_ref, pc_ref, *refs, n_pages, n_sub):
    del pt_ref
    n_in = n_sub * n_pages
    k_refs = [refs[u * n_pages:(u + 1) * n_pages] for u in range(n_sub)]
    v_refs = [refs[n_in + u * n_pages:n_in + (u + 1) * n_pages] for u in range(n_sub)]
    y_ref = refs[2 * n_in]
    s_len = q_ref.shape[0] // n_sub
    page = refs[0].shape[1]
    per_blk = MOBA_BLOCK // page
    n_past = n_pages // per_blk
    scale = HEAD_DIM ** -0.5
    hsl = [slice(h * HEAD_DIM, (h + 1) * HEAD_DIM) for h in range(C_HEADS)]
    probs_ = [(u, h) for u in range(n_sub) for h in range(C_HEADS)]
    qrows = lambda u: slice(u * s_len, (u + 1) * s_len)

    q_s = {(u, h): _split(q_ref[qrows(u), hsl[h]]) for (u, h) in probs_}
    s_rows = []
    for (u, h) in probs_:
        parts = [_split(k_refs[u][g][hsl[h], :]) for g in range(n_pages)]
        k_s = (jnp.concatenate([p[0] for p in parts], axis=1),
               jnp.concatenate([p[1] for p in parts], axis=1))
        s_rows.append(_mm3(q_s[(u, h)], k_s))
    s_all = jnp.concatenate(s_rows, axis=0)
    so = jnp.concatenate([_mm3(q_s[(u, h)], _split(kn_ref[qrows(u), hsl[h]]), _NT)
                          for (u, h) in probs_], axis=0)

    blocks = [s_all[:, b * MOBA_BLOCK:(b + 1) * MOBA_BLOCK] for b in range(n_past)]
    gates = [blk.sum(axis=-1, keepdims=True) * (1.0 / MOBA_BLOCK) for blk in blocks]
    sel = _topk_past_mask(gates, n_past)
    n_rows = len(probs_) * s_len
    row = lax.broadcasted_iota(jnp.int32, (n_rows, s_len), 0) % s_len
    col = lax.broadcasted_iota(jnp.int32, (n_rows, s_len), 1)
    so = jnp.where(col <= row, so * scale, NEG_BIG)
    pieces = [jnp.where(sel[b] > 0.5, blocks[b] * scale, NEG_BIG) for b in range(n_past)]
    m = so.max(axis=-1, keepdims=True)
    for b in range(n_past):
        m = jnp.maximum(m, pieces[b].max(axis=-1, keepdims=True))
    po = jnp.exp(so - m)
    den = po.sum(axis=-1, keepdims=True)
    probs = []
    for b in range(n_past):
        pb = jnp.exp(pieces[b] - m)
        den = den + pb.sum(axis=-1, keepdims=True)
        probs.append(pb.astype(BF16))
    p_all = jnp.concatenate(probs, axis=1)
    po = po.astype(BF16)

    for i, (u, h) in enumerate(probs_):
        rs = slice(i * s_len, (i + 1) * s_len)
        v_t = jnp.concatenate([v_refs[u][g][hsl[h], :].astype(BF16) for g in range(n_pages)],
                              axis=1)
        vn = pc_ref[qrows(u), 2 * C_WIDTH + h * HEAD_DIM:2 * C_WIDTH + (h + 1) * HEAD_DIM]
        z = pc_ref[qrows(u), 3 * C_WIDTH + h * HEAD_DIM:3 * C_WIDTH + (h + 1) * HEAD_DIM]
        acc = _dot_nt(p_all[rs], v_t) + _dot(po[rs], vn.astype(BF16))
        y_ref[qrows(u), hsl[h]] = (acc / den[rs] * _silu(z)).astype(y_ref.dtype)


def _moba_sample(page_table, qn, kn, pc, cache_kt, cache_vt, layer, n_seq, s_len, n_sub):
    n_pages = page_table.shape[1]
    page = cache_kt.shape[-1]
    assert MOBA_BLOCK % page == 0 and (n_pages * page) % MOBA_BLOCK == 0 and n_seq % n_sub == 0
    rows = n_sub * s_len
    cache_spec = lambda u, g: pl.BlockSpec(
        (None, None, C_WIDTH, page), lambda s, pt: (layer, pt[s * n_sub + u, g], 0, 0))
    cache_specs = [cache_spec(u, g) for u in range(n_sub) for g in range(n_pages)]
    grid_spec = pltpu.PrefetchScalarGridSpec(
        num_scalar_prefetch=1,
        grid=(n_seq // n_sub,),
        in_specs=[pl.BlockSpec((rows, C_WIDTH), lambda s, pt: (s, 0)),
                  pl.BlockSpec((rows, C_WIDTH), lambda s, pt: (s, 0)),
                  pl.BlockSpec((rows, C_COLS), lambda s, pt: (s, 0))] + cache_specs * 2,
        out_specs=pl.BlockSpec((rows, C_WIDTH), lambda s, pt: (s, 0)))
    n_in = n_sub * n_pages
    return pl.pallas_call(
        functools.partial(_moba_sample_kernel, n_pages=n_pages, n_sub=n_sub),
        grid_spec=grid_spec,
        out_shape=jax.ShapeDtypeStruct((n_seq * s_len, C_WIDTH), BF16),
        compiler_params=_cparams(("parallel",)),
        name="moba_sample",
    )(page_table, qn, kn, pc, *([cache_kt] * n_in), *([cache_vt] * n_in))


def _out_proj_kernel(x_ref, ya_ref, yb_ref, yc_ref, w_ref, o_ref):
    acc = _dot(ya_ref[...], w_ref[0:A_WIDTH, :])
    acc = acc + _dot(yb_ref[...], w_ref[A_WIDTH:A_WIDTH + B_WIDTH, :])
    acc = acc + _dot(yc_ref[...], w_ref[A_WIDTH + B_WIDTH:, :])
    o_ref[...] = x_ref[...] + acc


def _out_proj(x2d, ya, yb, yc, w_bf16, tm):
    t, d = x2d.shape
    assert t % tm == 0
    row = lambda w: pl.BlockSpec((tm, w), lambda i: (i, 0))
    return pl.pallas_call(
        _out_proj_kernel,
        grid=(t // tm,),
        in_specs=[row(d), row(A_WIDTH), row(B_WIDTH), row(C_WIDTH),
                  pl.BlockSpec(w_bf16.shape, lambda i: (0, 0))],
        out_specs=row(d),
        out_shape=jax.ShapeDtypeStruct((t, d), F32),
        compiler_params=_cparams(("parallel",)),
        name="out_proj",
    )(x2d, ya, yb, yc, w_bf16)


def _group_sum_matrix(width):
    g = jnp.arange(width) // HEAD_DIM
    return (g[:, None] == g[None, :]).astype(BF16)


def _rope_tables(pos):
    half = HEAD_DIM // 2
    inv = ROPE_THETA ** (-jnp.arange(half, dtype=F32) / half)
    ang = pos.astype(F32)[:, None] * inv[None, :]
    cos = jnp.cos(ang)
    sin = jnp.sin(ang)
    return jnp.tile(cos, (1, 4)), jnp.tile(jnp.concatenate([-sin, sin], axis=1), (1, 2))


def _tile_heads(g, n_heads):
    return jnp.tile(g.reshape(1, HEAD_DIM), (1, n_heads))


def kernel(x_prompt, x_sample, cache_k, cache_v, state_wkv, state_shift, page_table, norm_g, w_in, w_out, a_ln_g, a_ln_b, a_ws, a_bs, b_mu, b_w0, b_w2, b_a0, b_a2, b_kk, b_ka, b_rk, b_lnx_g, b_lnx_b, c_qn_g, c_kn_g):
    depth = w_in.shape[0]
    n_p, seq_len, d_model = x_prompt.shape
    n_s, s_len, _ = x_sample.shape
    page = cache_k.shape[2]
    past_len = page_table.shape[1] * page
    assert seq_len % MOBA_BLOCK == 0 and seq_len % A_CHUNK == 0 and seq_len % RWKV_CHUNK == 0
    assert past_len % MOBA_BLOCK == 0 and s_len % 8 == 0 and s_len <= min(A_CHUNK, RWKV_CHUNK)
    n_blk = seq_len // MOBA_BLOCK

    gsum_a = gsum_b = _group_sum_matrix(LANES)
    cos_p, sin_p = _rope_tables(jnp.arange(seq_len))
    cos_s, sin_s = _rope_tables(past_len + jnp.arange(n_s * s_len) % s_len)
    cache_kt = jnp.transpose(cache_k, (0, 1, 3, 4, 2)).reshape(depth, -1, C_WIDTH, page)
    cache_vt = jnp.transpose(cache_v, (0, 1, 3, 4, 2)).reshape(depth, -1, C_WIDTH, page)
    w_in_b = w_in.astype(BF16)
    w_out_b = w_out.astype(BF16)
    zero_shift = jnp.zeros((n_p, 1, B_SHIFT_W), F32)
    zero_wkv = jnp.zeros((n_p, B_HEADS, HEAD_DIM, HEAD_DIM), F32)
    seqs_per_tile = A_CHUNK // s_len
    eye_tile = jnp.eye(seqs_per_tile, dtype=F32)

    hp = x_prompt.reshape(n_p * seq_len, d_model)
    hs = x_sample.reshape(n_s * s_len, d_model)
    outs = {k: [] for k in ("kp", "vp", "ks", "vs", "wkvp", "wkvs", "shp", "shs", "gvs")}
    for l in range(depth):
        g = norm_g[l].reshape(1, d_model)
        lng_a = a_ln_g[l].reshape(1, A_WIDTH)
        lnb_a = a_ln_b[l].reshape(1, A_WIDTH)
        bs_full = jnp.repeat(a_bs[l].T, HEAD_DIM, axis=1)
        ws_s = jnp.stack([jnp.kron(eye_tile, a_ws[l, h, :s_len, :s_len]) for h in range(A_HEADS)])
        bs_s = jnp.tile(bs_full[:s_len], (seqs_per_tile, 1))
        prm = dict(mu=b_mu[l].reshape(1, -1), w0=b_w0[l].reshape(1, -1), w2=b_w2[l],
                   a0=b_a0[l].reshape(1, -1), a2=b_a2[l], kk=b_kk[l].reshape(1, -1),
                   ka=b_ka[l].reshape(1, -1), rk=b_rk[l].reshape(1, -1),
                   lnx_g=b_lnx_g[l].reshape(1, -1), lnx_b=b_lnx_b[l].reshape(1, -1))
        qg = _tile_heads(c_qn_g[l], C_HEADS)
        kg = _tile_heads(c_kn_g[l], C_HEADS)

        pa, pb, pc = _in_proj(hp, g, w_in_b[l], min(PROJ_ROWS, hp.shape[0]))
        ya, _ = _gmlp(pa, a_ws[l], bs_full, lng_a, lnb_a, gsum_a, GMLP_CHUNKS * A_CHUNK,
                      GMLP_CHUNKS)
        yb, wkvp, shp = _rwkv(pb, zero_shift, zero_wkv, prm, gsum_b, n_p,
                              math.gcd(RWKV_PROMPT_SEQS, n_p), min(RWKV_ROWS, seq_len), RWKV_CHUNK)
        kt, vt, yc = _moba_fused(pc, cos_p, sin_p, qg, kg, gsum_b, n_p, seq_len)
        hp = _out_proj(hp, ya, yb, yc, w_out_b[l], min(PROJ_ROWS, hp.shape[0]))
        outs["kp"].append(jnp.transpose(kt, (0, 3, 1, 2)))
        outs["vp"].append(jnp.transpose(vt, (0, 3, 1, 2)))
        outs["wkvp"].append(wkvp)
        outs["shp"].append(shp.reshape(n_p, B_SHIFT_W))

        pa, pb, pc = _in_proj(hs, g, w_in_b[l], min(PROJ_ROWS, hs.shape[0]))
        ya, gv = _gmlp(pa, ws_s, bs_s, lng_a, lnb_a, gsum_a, A_CHUNK, 1)
        yb, wkvs, shs = _rwkv(pb, state_shift[l].reshape(n_s, 1, B_SHIFT_W), state_wkv[l],
                              prm, gsum_b, n_s, math.gcd(RWKV_SAMPLE_SEQS, n_s), s_len, s_len)
        qn, kn = _prep_sample(pc, cos_s, sin_s, qg, kg, gsum_b, min(PREP_ROWS, hs.shape[0]))
        yc = _moba_sample(page_table, qn, kn, pc, cache_kt, cache_vt, l, n_s, s_len,
                          math.gcd(MOBA_SAMPLE_SEQS, n_s))
        hs = _out_proj(hs, ya, yb, yc, w_out_b[l], min(PROJ_ROWS, hs.shape[0]))
        outs["ks"].append(kn.reshape(n_s, s_len, C_HEADS, HEAD_DIM))
        outs["vs"].append(pc[:, 2 * C_WIDTH:3 * C_WIDTH].reshape(n_s, s_len, C_HEADS, HEAD_DIM))
        outs["wkvs"].append(wkvs)
        outs["shs"].append(shs.reshape(n_s, B_SHIFT_W))
        outs["gvs"].append(gv.reshape(n_s, s_len, A_WIDTH))

    st = {k: jnp.stack(v) for k, v in outs.items()}
    return (hp.reshape(n_p, seq_len, d_model), hs.reshape(n_s, s_len, d_model),
            st["kp"], st["vp"], st["ks"], st["vs"], st["wkvp"], st["wkvs"],
            st["shp"], st["shs"], st["gvs"])
```

```python
import functools
import math

import jax
import jax.numpy as jnp
from jax import lax
from jax.experimental import pallas as pl
from jax.experimental.pallas import tpu as pltpu

F32 = jnp.float32
BF16 = jnp.bfloat16

LANES = 128
HEAD_DIM = 64
A_HEADS = 4
A_WIDTH = A_HEADS * HEAD_DIM
A_CHUNK = 128
B_HEADS = 6
B_WIDTH = B_HEADS * HEAD_DIM
B_LORA = 64
B_SHIFT_W = 3 * B_WIDTH + 2 * B_LORA
C_HEADS = 6
C_WIDTH = C_HEADS * HEAD_DIM
MOBA_BLOCK = 256
MOBA_TOPK = 3
ROPE_THETA = 10000.0
A_COLS = 3 * A_WIDTH
B_COLS = B_SHIFT_W + B_WIDTH
C_COLS = 4 * C_WIDTH
RMS_EPS = 1e-6
LN_EPS = 1e-5
GN_EPS = 64e-5
NEG_BIG = -1e30
LOG2_E = 1.4426950408889634

VMEM_LIMIT_BYTES = 56 * 1024 * 1024
RWKV_CHUNK = 64
RWKV_ROWS = 256
PROJ_ROWS = 512
PROJ_ROWS_SAMPLE = 256
IN_PROJ_SUB = 128
GMLP_CHUNKS = 4
PREP_ROWS = 256
MOBA_SAMPLE_SEQS = 2
RWKV_PROMPT_SEQS = 4
RWKV_SAMPLE_SEQS = 8


def _cparams(sem):
    return pltpu.CompilerParams(dimension_semantics=sem,
                                vmem_limit_bytes=VMEM_LIMIT_BYTES)


def _dot(a, b):
    return jnp.dot(a, b, preferred_element_type=F32)


def _dot_nt(a, b):
    return lax.dot_general(a, b, (((1,), (1,)), ((), ())), preferred_element_type=F32)


def _silu(z):
    return z * jax.nn.sigmoid(z)


def _split(x):
    hi = x.astype(BF16)
    lo = (x - hi.astype(F32)).astype(BF16)
    return hi, lo


def _cat_rows(a, b):
    return jnp.concatenate([a[0], b[0]], axis=0), jnp.concatenate([a[1], b[1]], axis=0)


_NN = (((1,), (0,)), ((), ()))
_NT = (((1,), (1,)), ((), ()))
_TN = (((0,), (0,)), ((), ()))


def _mm3(a, b, dims=_NN):
    d = lambda x, y: lax.dot_general(x, y, dims, preferred_element_type=F32)
    return d(a[0], b[0]) + d(a[0], b[1]) + d(a[1], b[0])


def _mm_exact_rhs(x, g_bf16, passes=2):
    hi = x.astype(BF16)
    lo = (x - hi.astype(F32)).astype(BF16) if passes == 2 else None
    cols = []
    for p in range(x.shape[1] // LANES):
        cs = slice(p * LANES, (p + 1) * LANES)
        acc = _dot(hi[:, cs], g_bf16)
        if passes == 2:
            acc = acc + _dot(lo[:, cs], g_bf16)
        cols.append(acc)
    return jnp.concatenate(cols, axis=1)


def _in_proj_kernel(x_ref, g_ref, w_ref, pa_ref, pb_ref, pc_ref, *, n_sub):
    sub = x_ref.shape[0] // n_sub
    for j in range(n_sub):
        rs = slice(j * sub, (j + 1) * sub)
        x = x_ref[rs, :]
        ms = jnp.mean(x * x, axis=-1, keepdims=True)
        h = (x * lax.rsqrt(ms + RMS_EPS) * g_ref[...]).astype(BF16)
        p = _dot(h, w_ref[...])
        pa_ref[rs, :] = p[:, :A_COLS]
        pb_ref[rs, :] = p[:, A_COLS:A_COLS + B_COLS]
        pc_ref[rs, :] = p[:, A_COLS + B_COLS:]


def _in_proj(x2d, g, w_bf16, layer, tm):
    t, d = x2d.shape
    n = w_bf16.shape[2]
    assert t % tm == 0 and tm % IN_PROJ_SUB == 0
    return pl.pallas_call(
        functools.partial(_in_proj_kernel, n_sub=tm // IN_PROJ_SUB),
        grid=(t // tm,),
        in_specs=[pl.BlockSpec((tm, d), lambda i: (i, 0)),
                  pl.BlockSpec((1, d), lambda i: (0, 0)),
                  pl.BlockSpec((None, d, n), lambda i: (layer, 0, 0))],
        out_specs=[pl.BlockSpec((tm, A_COLS), lambda i: (i, 0)),
                   pl.BlockSpec((tm, B_COLS), lambda i: (i, 0)),
                   pl.BlockSpec((tm, C_COLS), lambda i: (i, 0))],
        out_shape=[jax.ShapeDtypeStruct((t, A_COLS), F32),
                   jax.ShapeDtypeStruct((t, B_COLS), F32),
                   jax.ShapeDtypeStruct((t, C_COLS), F32)],
        compiler_params=_cparams(("parallel",)),
        name="in_proj",
    )(x2d, g, w_bf16)


def _gmlp_kernel(pa_ref, ws_ref, bs_ref, lng_ref, lnb_ref, gsum_ref, ya_ref, vn_ref,
                 *, n_sub):
    rows = pa_ref.shape[0]
    sub = rows // n_sub
    u = pa_ref[:, 0:A_WIDTH]
    v = pa_ref[:, A_WIDTH:2 * A_WIDTH]
    z = pa_ref[:, 2 * A_WIDTH:3 * A_WIDTH]
    gsum = gsum_ref[...]
    mu = _mm_exact_rhs(v, gsum) * (1.0 / HEAD_DIM)
    d = v - mu
    var = _mm_exact_rhs(d * d, gsum) * (1.0 / HEAD_DIM)
    vn = d * lax.rsqrt(var + LN_EPS) * lng_ref[...] + lnb_ref[...]
    vn_ref[...] = vn
    ri = lax.broadcasted_iota(jnp.int32, (sub, sub), 0)
    ci = lax.broadcasted_iota(jnp.int32, (sub, sub), 1)
    causal = ri >= ci
    vn_b = vn.astype(BF16)
    for h in range(A_HEADS):
        wm = jnp.where(causal, ws_ref[h], 0.0).astype(BF16)
        sl = slice(h * HEAD_DIM, (h + 1) * HEAD_DIM)
        for c in range(n_sub):
            rs = slice(c * sub, (c + 1) * sub)
            mixed = _dot(wm, vn_b[rs, sl]) + bs_ref[:, sl]
            ya_ref[rs, sl] = (u[rs, sl] * mixed * _silu(z[rs, sl])).astype(ya_ref.dtype)


def _gmlp(pa, ws_eff, bs_eff, lng, lnb, gsum, rows, n_sub):
    t = pa.shape[0]
    sub = rows // n_sub
    assert t % rows == 0 and ws_eff.shape == (A_HEADS, sub, sub)
    return pl.pallas_call(
        functools.partial(_gmlp_kernel, n_sub=n_sub),
        grid=(t // rows,),
        in_specs=[pl.BlockSpec((rows, A_COLS), lambda i: (i, 0)),
                  pl.BlockSpec((A_HEADS, sub, sub), lambda i: (0, 0, 0)),
                  pl.BlockSpec((sub, A_WIDTH), lambda i: (0, 0)),
                  pl.BlockSpec((1, A_WIDTH), lambda i: (0, 0)),
                  pl.BlockSpec((1, A_WIDTH), lambda i: (0, 0)),
                  pl.BlockSpec((LANES, LANES), lambda i: (0, 0))],
        out_specs=[pl.BlockSpec((rows, A_WIDTH), lambda i: (i, 0)),
                   pl.BlockSpec((rows, A_WIDTH), lambda i: (i, 0))],
        out_shape=[jax.ShapeDtypeStruct((t, A_WIDTH), BF16),
                   jax.ShapeDtypeStruct((t, A_WIDTH), F32)],
        compiler_params=_cparams(("parallel",)),
        name="gmlp",
    )(pa, ws_eff, bs_eff, lng, lnb, gsum)


def _rwkv_kernel(pb_ref, shift0_ref, wkv0_ref, mu_ref, w0_ref, w2_ref, a0_ref, a2_ref,
                 kk_ref, ka_ref, rk_ref, lng_ref, lnb_ref, gsum_ref,
                 yb_ref, wkv_ref, shift_ref, state_scr, prev_scr, o_scr, *, chunk, n_sub):
    c = pl.program_id(1)
    n_c = pl.num_programs(1)
    sub_rows = pb_ref.shape[1]
    rows = n_sub * sub_rows
    n_ch = sub_rows // chunk

    @pl.when(c == 0)
    def _():
        state_scr[...] = wkv0_ref[...]
        prev_scr[...] = shift0_ref[...]

    sb = jnp.concatenate([pb_ref[q, :, 0:B_SHIFT_W] for q in range(n_sub)], axis=0)
    zb = jnp.concatenate([pb_ref[q, :, B_SHIFT_W:B_COLS] for q in range(n_sub)], axis=0)
    row_id = lax.broadcasted_iota(jnp.int32, (rows, 1), 0)
    prev = pltpu.roll(sb, 1, axis=0)
    last_rows = []
    for q in range(n_sub):
        prev = jnp.where(row_id == q * sub_rows, prev_scr[q], prev)
        last_rows.append(sb[(q + 1) * sub_rows - 1:(q + 1) * sub_rows, :])
        prev_scr[q] = last_rows[q]
    xs = sb + (prev - sb) * mu_ref[...]
    r = xs[:, 0:B_WIDTH]
    k = xs[:, B_WIDTH:2 * B_WIDTH]
    v = xs[:, 2 * B_WIDTH:3 * B_WIDTH]
    wd = xs[:, 3 * B_WIDTH:3 * B_WIDTH + B_LORA]
    ad = xs[:, 3 * B_WIDTH + B_LORA:B_SHIFT_W]

    gsum = gsum_ref[...]
    y = -(w0_ref[...] + _dot(jnp.tanh(wd).astype(BF16), w2_ref[...].astype(BF16)))
    softplus = jnp.maximum(y, 0.0) + jnp.log(1.0 + jnp.exp(-jnp.abs(y)))
    logdecay = -jnp.exp(-softplus - 0.5)
    a = jax.nn.sigmoid(a0_ref[...] + _dot(ad.astype(BF16), a2_ref[...].astype(BF16)))
    kk = k * kk_ref[...]
    kk = kk * lax.rsqrt(jnp.maximum(_mm_exact_rhs(kk * kk, gsum, 1), 1e-24))
    k2 = k * (1.0 + (a - 1.0) * ka_ref[...])

    ri = lax.broadcasted_iota(jnp.int32, (chunk, chunk), 0)
    ci = lax.broadcasted_iota(jnp.int32, (chunk, chunk), 1)
    tri = jnp.where(ri >= ci, 1.0, 0.0).astype(BF16)
    cums, tots = [], []
    for j in range(rows // chunk):
        ld = logdecay[j * chunk:(j + 1) * chunk]
        l1 = ld.astype(BF16)
        rem = ld - l1.astype(F32)
        l2 = rem.astype(BF16)
        l3 = (rem - l2.astype(F32)).astype(BF16)
        cj = _dot(tri, l1) + _dot(tri, l2) + _dot(tri, l3)
        cums.append(cj)
        tots.append(jnp.broadcast_to(cj[chunk - 1:chunk, :], cj.shape))
    cum = jnp.concatenate(cums, axis=0)
    tot = jnp.concatenate(tots, axis=0)
    e_neg = jnp.exp(-cum)
    tail = jnp.exp(tot - cum)
    g_tot = jnp.exp(tot)
    kka = kk * a
    at = (-kk * jnp.exp(cum - logdecay)).astype(BF16)
    rt = (r * jnp.exp(cum)).astype(BF16)
    bp = (kka * e_neg).astype(BF16)
    kp = (k2 * e_neg).astype(BF16)
    bh = (kka * tail).astype(BF16)
    kh = (k2 * tail).astype(BF16)
    vb = v.astype(BF16)

    li = lax.broadcasted_iota(jnp.int32, (2 * chunk, 2 * chunk), 0)
    lj = lax.broadcasted_iota(jnp.int32, (2 * chunk, 2 * chunk), 1) % chunk
    lower4 = jnp.where(li < chunk, li - 1, li - chunk) >= lj
    xi = lax.broadcasted_iota(jnp.int32, (chunk, 2 * chunk), 0)
    xj = lax.broadcasted_iota(jnp.int32, (chunk, 2 * chunk), 1)
    left = xj < chunk
    eye_r = (xj - chunk == xi).astype(F32)
    n_lvl = max(1, int(math.ceil(math.log2(chunk))))
    mm = lambda x, y, dims=_NN: lax.dot_general(x, y, dims, preferred_element_type=F32)
    cat = lambda x, y: jnp.concatenate([x, y], axis=0)

    hsl = [slice(h * HEAD_DIM, (h + 1) * HEAD_DIM) for h in range(B_HEADS)]
    chains = [(h, q) for q in range(n_sub) for h in range(B_HEADS)]
    rsl = lambda t: slice(t[1] * sub_rows + t[2] * chunk, t[1] * sub_rows + (t[2] + 1) * chunk)
    cut = lambda x, t: x[rsl(t), hsl[t[0]]]
    zero_v = jnp.zeros((chunk, HEAD_DIM), BF16)
    v_h, y_m, wr, pm_b, bk, qv = {}, {}, {}, {}, {}, {}
    state = {(h, q): state_scr[q, h] for (h, q) in chains}

    def independent(chs):
        trips = [cq + (ch,) for ch in chs for cq in chains]
        at_h = {t: cut(at, t) for t in trips}
        rt_h = {t: cut(rt, t) for t in trips}
        for t in trips:
            v_h[t] = cut(vb, t)
        g = {t: jnp.where(lower4, mm(cat(at_h[t], rt_h[t]), cat(cut(bp, t), cut(kp, t)), _NT), 0.0)
             for t in trips}
        yield
        bq_v = {t: mm(g[t].astype(BF16), cat(zero_v, v_h[t])) for t in trips}
        yield
        x = {t: jnp.where(left, g[t][:chunk], eye_r) for t in trips}
        for _ in range(n_lvl):
            res = {t: mm(x[t][:, :chunk].astype(BF16), x[t].astype(BF16)) for t in trips}
            x = {t: res[t] + jnp.where(left, 0.0, x[t]) for t in trips}
            yield
        wy = {t: mm(x[t][:, chunk:].astype(BF16),
                    jnp.concatenate([at_h[t].astype(F32), bq_v[t][:chunk]], axis=1).astype(BF16))
              for t in trips}
        for t in trips:
            y_m[t] = wy[t][:, HEAD_DIM:]
            wr[t] = cat(wy[t][:, :HEAD_DIM].astype(BF16), rt_h[t])
            pm_b[t] = g[t][chunk:, :chunk].astype(BF16)
            bk[t] = cat(cut(bh, t), cut(kh, t))
            qv[t] = bq_v[t][chunk:]
        yield

    def recurrent(ch):
        ws = {cq: mm(wr[cq + (ch,)], state[cq].astype(BF16), _NT) for cq in chains}
        yield
        u_b = {cq: (ws[cq][:chunk] + y_m[cq + (ch,)]).astype(BF16) for cq in chains}
        for cq in chains:
            t = cq + (ch,)
            r0 = rsl(t).start
            state[cq] = (state[cq] * g_tot[r0:r0 + 1, hsl[cq[0]]]
                         + mm(cat(u_b[cq], v_h[t]), bk[t], _TN))
        yield
        for cq in chains:
            t = cq + (ch,)
            o_scr[rsl(t), hsl[cq[0]]] = ws[cq][chunk:] + mm(pm_b[t], u_b[cq]) + qv[t]
        yield

    for _ in independent(range(n_ch)):
        pass
    for ch in range(n_ch):
        for _ in recurrent(ch):
            pass
    for (h, q) in chains:
        state_scr[q, h] = state[(h, q)]

    o = o_scr[...]
    m = _mm_exact_rhs(o, gsum, 1) * (1.0 / HEAD_DIM)
    d = o - m
    var = _mm_exact_rhs(d * d, gsum, 1) * (1.0 / HEAD_DIM)
    on = d * lax.rsqrt(var + GN_EPS) * lng_ref[...] + lnb_ref[...]
    bonus = _mm_exact_rhs(r * k2 * rk_ref[...], gsum, 1) * v
    yb = ((on + bonus) * _silu(zb)).astype(yb_ref.dtype)
    for q in range(n_sub):
        yb_ref[q] = yb[q * sub_rows:(q + 1) * sub_rows]

    @pl.when(c == n_c - 1)
    def _():
        wkv_ref[...] = state_scr[...]
        for q in range(n_sub):
            shift_ref[q] = last_rows[q]


def _rwkv(pb, shift0, wkv0, prm, gsum, n_seq, n_sub, sub_rows, chunk):
    t = pb.shape[0]
    seq_rows = t // n_seq
    rows = n_sub * sub_rows
    n_c = seq_rows // sub_rows
    assert n_c * n_seq * sub_rows == t and sub_rows % chunk == 0 and n_seq % n_sub == 0
    n_g = n_seq // n_sub
    vec = lambda w: pl.BlockSpec((1, w), lambda s, c: (0, 0))
    yb, wkv, shift = pl.pallas_call(
        functools.partial(_rwkv_kernel, chunk=chunk, n_sub=n_sub),
        grid=(n_g, n_c),
        in_specs=[pl.BlockSpec((n_sub, sub_rows, B_COLS), lambda s, c: (s, c, 0)),
                  pl.BlockSpec((n_sub, 1, B_SHIFT_W), lambda s, c: (s, 0, 0)),
                  pl.BlockSpec((n_sub, B_HEADS, HEAD_DIM, HEAD_DIM), lambda s, c: (s, 0, 0, 0)),
                  vec(B_SHIFT_W), vec(B_WIDTH),
                  pl.BlockSpec((B_LORA, B_WIDTH), lambda s, c: (0, 0)),
                  vec(B_WIDTH),
                  pl.BlockSpec((B_LORA, B_WIDTH), lambda s, c: (0, 0)),
                  vec(B_WIDTH), vec(B_WIDTH), vec(B_WIDTH), vec(B_WIDTH), vec(B_WIDTH),
                  pl.BlockSpec((LANES, LANES), lambda s, c: (0, 0))],
        out_specs=[pl.BlockSpec((n_sub, sub_rows, B_WIDTH), lambda s, c: (s, c, 0)),
                   pl.BlockSpec((n_sub, B_HEADS, HEAD_DIM, HEAD_DIM), lambda s, c: (s, 0, 0, 0)),
                   pl.BlockSpec((n_sub, 1, B_SHIFT_W), lambda s, c: (s, 0, 0))],
        out_shape=[jax.ShapeDtypeStruct((n_seq, seq_rows, B_WIDTH), BF16),
                   jax.ShapeDtypeStruct((n_seq, B_HEADS, HEAD_DIM, HEAD_DIM), F32),
                   jax.ShapeDtypeStruct((n_seq, 1, B_SHIFT_W), F32)],
        scratch_shapes=[pltpu.VMEM((n_sub, B_HEADS, HEAD_DIM, HEAD_DIM), F32),
                        pltpu.VMEM((n_sub, 1, B_SHIFT_W), F32),
                        pltpu.VMEM((rows, B_WIDTH), F32)],
        compiler_params=_cparams(("parallel", "arbitrary")),
        name="rwkv",
    )(pb.reshape(n_seq, seq_rows, B_COLS), shift0, wkv0, prm["mu"], prm["w0"], prm["w2"],
      prm["a0"], prm["a2"], prm["kk"], prm["ka"], prm["rk"], prm["lnx_g"], prm["lnx_b"], gsum)
    return yb.reshape(t, B_WIDTH), wkv, shift


def _qk_norm_rope(x, g, cos, sin_signed, gsum):
    ms = _mm_exact_rhs(x * x, gsum) * (1.0 / HEAD_DIM)
    y = x * lax.rsqrt(ms + RMS_EPS) * g
    half = HEAD_DIM // 2
    lane = lax.broadcasted_iota(jnp.int32, (1, LANES), 1)
    first = (lane % HEAD_DIM) < half
    outs = []
    for p in range(x.shape[1] // LANES):
        yp = y[:, p * LANES:(p + 1) * LANES]
        partner = jnp.where(first, pltpu.roll(yp, LANES - half, axis=1),
                            pltpu.roll(yp, half, axis=1))
        outs.append(yp * cos + partner * sin_signed)
    return outs


def _prep_sample_kernel(pc_ref, cos_ref, sin_ref, qg_ref, kg_ref, gsum_ref, q_ref, k_ref):
    gsum = gsum_ref[...]
    cos = cos_ref[...]
    sin = sin_ref[...]
    qs = _qk_norm_rope(pc_ref[:, 0:C_WIDTH], qg_ref[...], cos, sin, gsum)
    ks = _qk_norm_rope(pc_ref[:, C_WIDTH:2 * C_WIDTH], kg_ref[...], cos, sin, gsum)
    for p in range(C_HEADS // 2):
        q_ref[:, p * LANES:(p + 1) * LANES] = qs[p]
        k_ref[:, p * LANES:(p + 1) * LANES] = ks[p]


def _prep_sample(pc, cos, sin, qg, kg, gsum, tm):
    t = pc.shape[0]
    assert t % tm == 0
    return pl.pallas_call(
        _prep_sample_kernel,
        grid=(t // tm,),
        in_specs=[pl.BlockSpec((tm, C_COLS), lambda i: (i, 0)),
                  pl.BlockSpec((tm, LANES), lambda i: (i, 0)),
                  pl.BlockSpec((tm, LANES), lambda i: (i, 0)),
                  pl.BlockSpec((1, C_WIDTH), lambda i: (0, 0)),
                  pl.BlockSpec((1, C_WIDTH), lambda i: (0, 0)),
                  pl.BlockSpec((LANES, LANES), lambda i: (0, 0))],
        out_specs=[pl.BlockSpec((tm, C_WIDTH), lambda i: (i, 0)),
                   pl.BlockSpec((tm, C_WIDTH), lambda i: (i, 0))],
        out_shape=[jax.ShapeDtypeStruct((t, C_WIDTH), F32),
                   jax.ShapeDtypeStruct((t, C_WIDTH), F32)],
        compiler_params=_cparams(("parallel",)),
        name="moba_prep_sample",
    )(pc, cos, sin, qg, kg, gsum)


def _topk_past_mask(gates, n_valid):
    n = len(gates)
    valid = [jnp.where(j < n_valid, 1.0, 0.0).astype(F32) for j in range(n)]
    sel = []
    for j in range(n):
        rank = jnp.zeros(gates[j].shape, F32)
        for j2 in range(n):
            if j2 == j:
                continue
            beats = (gates[j2] > gates[j]) if j2 > j else (gates[j2] >= gates[j])
            rank = rank + jnp.where(beats, valid[j2], 0.0)
        sel.append(jnp.where(rank < MOBA_TOPK, valid[j], 0.0))
    return sel


def _moba_fused_kernel(pc_ref, cos_ref, sin_ref, qg_ref, kg_ref, gsum_ref, *refs, layer):
    if layer:
        kt_prev_ref, vt_prev_ref = refs[:2]
        refs = refs[2:]
    kt_all_ref, vt_all_ref, y_ref, kr_scr, vb_scr, kmean_scr, m_scr, l_scr, acc_scr = refs
    for j in range(layer):
        kt_all_ref[j] = kt_prev_ref[j]
        vt_all_ref[j] = vt_prev_ref[j]
    kt_ref = kt_all_ref.at[layer]
    vt_ref = vt_all_ref.at[layer]
    _moba_fused_body(pc_ref, cos_ref, sin_ref, qg_ref, kg_ref, gsum_ref, kt_ref, vt_ref, y_ref,
                     kr_scr, vb_scr, kmean_scr, m_scr, l_scr, acc_scr)


def _moba_fused_body(pc_ref, cos_ref, sin_ref, qg_ref, kg_ref, gsum_ref,
                     kt_ref, vt_ref, y_ref,
                     kr_scr, vb_scr, kmean_scr, m_scr, l_scr, acc_scr):
    i = pl.program_id(1)
    n_blk = kr_scr.shape[0]
    tq = pc_ref.shape[0]
    scale = HEAD_DIM ** -0.5
    heads = range(C_HEADS)
    hsl = [slice(h * HEAD_DIM, (h + 1) * HEAD_DIM) for h in heads]

    gsum = gsum_ref[...]
    cos = cos_ref[...]
    sin = sin_ref[...]
    qs = _qk_norm_rope(pc_ref[:, 0:C_WIDTH], qg_ref[...], cos, sin, gsum)
    ks = _qk_norm_rope(pc_ref[:, C_WIDTH:2 * C_WIDTH], kg_ref[...], cos, sin, gsum)
    @pl.when(i == 0)
    def _():
        kmean_scr[...] = jnp.zeros(kmean_scr.shape, F32)

    mean_row = jnp.concatenate([jnp.mean(ks[p], axis=0, keepdims=True)
                                for p in range(C_HEADS // 2)], axis=1)
    blk_row = lax.broadcasted_iota(jnp.int32, kmean_scr.shape, 0)
    kmean = jnp.where(blk_row == i, mean_row, kmean_scr[...])
    kmean_scr[...] = kmean

    qt, k_own, v_own = [], [], []
    for p in range(C_HEADS // 2):
        vp = pc_ref[:, 2 * C_WIDTH + p * LANES:2 * C_WIDTH + (p + 1) * LANES]
        qpt = qs[p].T
        kpt = ks[p].T
        vpt = vp.T
        for j in range(2):
            h = 2 * p + j
            ds_ = slice(j * HEAD_DIM, (j + 1) * HEAD_DIM)
            qt.append(qpt[ds_, :])
            kt_ref[h] = kpt[ds_, :]
            vt_ref[h] = vpt[ds_, :]
            k_own.append(ks[p][:, ds_].astype(BF16))
            v_own.append(vpt[ds_, :].astype(BF16))
            kr_scr[i, h] = k_own[h]
            vb_scr[i, h] = v_own[h]

    own_blk = jnp.full((1, tq), i, jnp.int32)
    gate = [_mm3(_split(kmean[:, hsl[h]]), _split(qt[h])) for h in heads]
    bias = []
    for h in heads:
        sel = _topk_past_mask([gate[h][b:b + 1, :] for b in range(n_blk)], own_blk)
        bias.append([jnp.where(sel[b] > 0.5, 0.0, NEG_BIG) for b in range(n_blk - 1)])
    qb = [(qt[h] * (scale * LOG2_E)).astype(BF16) for h in heads]

    key = lax.broadcasted_iota(jnp.int32, (MOBA_BLOCK, tq), 0)
    qry = lax.broadcasted_iota(jnp.int32, (MOBA_BLOCK, tq), 1)
    s = [jnp.where(key <= qry, _dot(k_own[h], qb[h]), NEG_BIG) for h in heads]
    m = [s[h].max(axis=0, keepdims=True) for h in heads]
    p = [jnp.exp2(s[h] - m[h]) for h in heads]
    pv = [_dot(v_own[h], p[h].astype(BF16)) for h in heads]
    for h in heads:
        m_scr[h] = m[h]
        l_scr[h] = p[h].sum(axis=0, keepdims=True)
        acc_scr[h] = pv[h]

    for b in range(n_blk - 1):
        @pl.when(b < i)
        def _(b=b):
            s = [_dot(kr_scr[b, h], qb[h]) + bias[h][b] for h in heads]
            m_old = [m_scr[h] for h in heads]
            m_new = [jnp.maximum(m_old[h], s[h].max(axis=0, keepdims=True)) for h in heads]
            p = [jnp.exp2(s[h] - m_new[h]) for h in heads]
            pv = [_dot(vb_scr[b, h], p[h].astype(BF16)) for h in heads]
            for h in heads:
                alpha = jnp.exp2(m_old[h] - m_new[h])
                m_scr[h] = m_new[h]
                l_scr[h] = l_scr[h] * alpha + p[h].sum(axis=0, keepdims=True)
                acc_scr[h] = acc_scr[h] * alpha + pv[h]

    for pr in range(C_HEADS // 2):
        out_t = jnp.concatenate([acc_scr[2 * pr + j] / l_scr[2 * pr + j] for j in range(2)], axis=0)
        cs = slice(pr * LANES, (pr + 1) * LANES)
        z = pc_ref[:, 3 * C_WIDTH + pr * LANES:3 * C_WIDTH + (pr + 1) * LANES]
        y_ref[:, cs] = (out_t.T * _silu(z)).astype(y_ref.dtype)


def _moba_fused(pc, cos, sin, qg, kg, gsum, n_seq, seq_len, kv_prev):
    tm = MOBA_BLOCK
    n_q = seq_len // tm
    assert n_q * tm == seq_len
    layer = 0 if kv_prev is None else kv_prev[0].shape[0]
    tr = lambda n: pl.BlockSpec((n, None, C_HEADS, HEAD_DIM, tm), lambda s, i: (0, s, 0, 0, i))
    kv_shape = jax.ShapeDtypeStruct((layer + 1, n_seq, C_HEADS, HEAD_DIM, seq_len), F32)
    return pl.pallas_call(
        functools.partial(_moba_fused_kernel, layer=layer),
        grid=(n_seq, n_q),
        in_specs=[pl.BlockSpec((tm, C_COLS), lambda s, i: (s * n_q + i, 0)),
                  pl.BlockSpec((tm, LANES), lambda s, i: (i, 0)),
                  pl.BlockSpec((tm, LANES), lambda s, i: (i, 0)),
                  pl.BlockSpec((1, C_WIDTH), lambda s, i: (0, 0)),
                  pl.BlockSpec((1, C_WIDTH), lambda s, i: (0, 0)),
                  pl.BlockSpec((LANES, LANES), lambda s, i: (0, 0))]
                 + ([tr(layer), tr(layer)] if layer else []),
        out_specs=[tr(layer + 1), tr(layer + 1),
                   pl.BlockSpec((tm, C_WIDTH), lambda s, i: (s * n_q + i, 0))],
        out_shape=[kv_shape, kv_shape,
                   jax.ShapeDtypeStruct((n_seq * seq_len, C_WIDTH), BF16)],
        scratch_shapes=[pltpu.VMEM((n_q, C_HEADS, tm, HEAD_DIM), BF16),
                        pltpu.VMEM((n_q, C_HEADS, HEAD_DIM, tm), BF16),
                        pltpu.VMEM((-(-n_q // 8) * 8, C_WIDTH), F32),
                        pltpu.VMEM((C_HEADS, 1, tm), F32),
                        pltpu.VMEM((C_HEADS, 1, tm), F32),
                        pltpu.VMEM((C_HEADS, HEAD_DIM, tm), F32)],
        compiler_params=_cparams(("parallel", "arbitrary")),
        name="moba_prompt",
    )(pc, cos, sin, qg, kg, gsum, *(kv_prev or ()))


def _moba_sample_kernel(pt_ref, q_ref, kn_ref, pc_ref, *refs, n_pages, n_sub):
    del pt_ref
    n_in = n_sub * n_pages
    k_refs = [refs[u * n_pages:(u + 1) * n_pages] for u in range(n_sub)]
    v_refs = [refs[n_in + u * n_pages:n_in + (u + 1) * n_pages] for u in range(n_sub)]
    y_ref = refs[2 * n_in]
    s_len = q_ref.shape[0] // n_sub
    page = refs[0].shape[1]
    per_blk = MOBA_BLOCK // page
    n_past = n_pages // per_blk
    scale = HEAD_DIM ** -0.5
    hsl = [slice(h * HEAD_DIM, (h + 1) * HEAD_DIM) for h in range(C_HEADS)]
    probs_ = [(u, h) for u in range(n_sub) for h in range(C_HEADS)]
    qrows = lambda u: slice(u * s_len, (u + 1) * s_len)

    q_s = {(u, h): _split(q_ref[qrows(u), hsl[h]]) for (u, h) in probs_}
    s_rows = []
    for (u, h) in probs_:
        parts = [_split(k_refs[u][g][hsl[h], :]) for g in range(n_pages)]
        k_s = (jnp.concatenate([p[0] for p in parts], axis=1),
               jnp.concatenate([p[1] for p in parts], axis=1))
        s_rows.append(_mm3(q_s[(u, h)], k_s))
    s_all = jnp.concatenate(s_rows, axis=0)
    so = jnp.concatenate([_mm3(q_s[(u, h)], _split(kn_ref[qrows(u), hsl[h]]), _NT)
                          for (u, h) in probs_], axis=0)

    blocks = [s_all[:, b * MOBA_BLOCK:(b + 1) * MOBA_BLOCK] for b in range(n_past)]
    gates = [blk.sum(axis=-1, keepdims=True) * (1.0 / MOBA_BLOCK) for blk in blocks]
    sel = _topk_past_mask(gates, n_past)
    n_rows = len(probs_) * s_len
    row = lax.broadcasted_iota(jnp.int32, (n_rows, s_len), 0) % s_len
    col = lax.broadcasted_iota(jnp.int32, (n_rows, s_len), 1)
    so = jnp.where(col <= row, so * scale, NEG_BIG)
    pieces = [jnp.where(sel[b] > 0.5, blocks[b] * scale, NEG_BIG) for b in range(n_past)]
    m = so.max(axis=-1, keepdims=True)
    for b in range(n_past):
        m = jnp.maximum(m, pieces[b].max(axis=-1, keepdims=True))
    po = jnp.exp(so - m)
    den = po.sum(axis=-1, keepdims=True)
    probs = []
    for b in range(n_past):
        pb = jnp.exp(pieces[b] - m)
        den = den + pb.sum(axis=-1, keepdims=True)
        probs.append(pb.astype(BF16))
    p_all = jnp.concatenate(probs, axis=1)
    po = po.astype(BF16)

    for i, (u, h) in enumerate(probs_):
        rs = slice(i * s_len, (i + 1) * s_len)
        v_t = jnp.concatenate([v_refs[u][g][hsl[h], :].astype(BF16) for g in range(n_pages)],
                              axis=1)
        vn = pc_ref[qrows(u), 2 * C_WIDTH + h * HEAD_DIM:2 * C_WIDTH + (h + 1) * HEAD_DIM]
        z = pc_ref[qrows(u), 3 * C_WIDTH + h * HEAD_DIM:3 * C_WIDTH + (h + 1) * HEAD_DIM]
        acc = _dot_nt(p_all[rs], v_t) + _dot(po[rs], vn.astype(BF16))
        y_ref[qrows(u), hsl[h]] = (acc / den[rs] * _silu(z)).astype(y_ref.dtype)


def _moba_sample(page_table, qn, kn, pc, cache_kt, cache_vt, layer, n_seq, s_len, n_sub):
    n_pages = page_table.shape[1]
    page = cache_kt.shape[-1]
    assert MOBA_BLOCK % page == 0 and (n_pages * page) % MOBA_BLOCK == 0 and n_seq % n_sub == 0
    rows = n_sub * s_len
    cache_spec = lambda u, g: pl.BlockSpec(
        (None, None, C_WIDTH, page), lambda s, pt: (layer, pt[s * n_sub + u, g], 0, 0))
    cache_specs = [cache_spec(u, g) for u in range(n_sub) for g in range(n_pages)]
    grid_spec = pltpu.PrefetchScalarGridSpec(
        num_scalar_prefetch=1,
        grid=(n_seq // n_sub,),
        in_specs=[pl.BlockSpec((rows, C_WIDTH), lambda s, pt: (s, 0)),
                  pl.BlockSpec((rows, C_WIDTH), lambda s, pt: (s, 0)),
                  pl.BlockSpec((rows, C_COLS), lambda s, pt: (s, 0))] + cache_specs * 2,
        out_specs=pl.BlockSpec((rows, C_WIDTH), lambda s, pt: (s, 0)))
    n_in = n_sub * n_pages
    return pl.pallas_call(
        functools.partial(_moba_sample_kernel, n_pages=n_pages, n_sub=n_sub),
        grid_spec=grid_spec,
        out_shape=jax.ShapeDtypeStruct((n_seq * s_len, C_WIDTH), BF16),
        compiler_params=_cparams(("parallel",)),
        name="moba_sample",
    )(page_table, qn, kn, pc, *([cache_kt] * n_in), *([cache_vt] * n_in))


def _out_proj_kernel(x_ref, ya_ref, yb_ref, yc_ref, w_ref, o_ref):
    acc = _dot(ya_ref[...], w_ref[0:A_WIDTH, :])
    acc = acc + _dot(yb_ref[...], w_ref[A_WIDTH:A_WIDTH + B_WIDTH, :])
    acc = acc + _dot(yc_ref[...], w_ref[A_WIDTH + B_WIDTH:, :])
    o_ref[...] = x_ref[...] + acc


def _out_proj(x2d, ya, yb, yc, w_bf16, layer, tm):
    t, d = x2d.shape
    assert t % tm == 0
    row = lambda w: pl.BlockSpec((tm, w), lambda i: (i, 0))
    return pl.pallas_call(
        _out_proj_kernel,
        grid=(t // tm,),
        in_specs=[row(d), row(A_WIDTH), row(B_WIDTH), row(C_WIDTH),
                  pl.BlockSpec((None,) + w_bf16.shape[1:], lambda i: (layer, 0, 0))],
        out_specs=row(d),
        out_shape=jax.ShapeDtypeStruct((t, d), F32),
        compiler_params=_cparams(("parallel",)),
        name="out_proj",
    )(x2d, ya, yb, yc, w_bf16)


def _group_sum_matrix(width):
    g = jnp.arange(width) // HEAD_DIM
    return (g[:, None] == g[None, :]).astype(BF16)


def _rope_tables(pos):
    half = HEAD_DIM // 2
    inv = ROPE_THETA ** (-jnp.arange(half, dtype=F32) / half)
    ang = pos.astype(F32)[:, None] * inv[None, :]
    cos = jnp.cos(ang)
    sin = jnp.sin(ang)
    return jnp.tile(cos, (1, 4)), jnp.tile(jnp.concatenate([-sin, sin], axis=1), (1, 2))


def _tile_heads(g, n_heads):
    return jnp.tile(g.reshape(1, HEAD_DIM), (1, n_heads))


def kernel(x_prompt, x_sample, cache_k, cache_v, state_wkv, state_shift, page_table, norm_g, w_in, w_out, a_ln_g, a_ln_b, a_ws, a_bs, b_mu, b_w0, b_w2, b_a0, b_a2, b_kk, b_ka, b_rk, b_lnx_g, b_lnx_b, c_qn_g, c_kn_g):
    depth = w_in.shape[0]
    n_p, seq_len, d_model = x_prompt.shape
    n_s, s_len, _ = x_sample.shape
    page = cache_k.shape[2]
    past_len = page_table.shape[1] * page
    assert seq_len % MOBA_BLOCK == 0 and seq_len % A_CHUNK == 0 and seq_len % RWKV_CHUNK == 0
    assert past_len % MOBA_BLOCK == 0 and s_len % 8 == 0 and s_len <= min(A_CHUNK, RWKV_CHUNK)
    n_blk = seq_len // MOBA_BLOCK

    gsum_a = gsum_b = _group_sum_matrix(LANES)
    cos_p, sin_p = _rope_tables(jnp.arange(seq_len))
    cos_s, sin_s = _rope_tables(past_len + jnp.arange(n_s * s_len) % s_len)
    cache_kt = jnp.transpose(cache_k, (0, 1, 3, 4, 2)).reshape(depth, -1, C_WIDTH, page)
    cache_vt = jnp.transpose(cache_v, (0, 1, 3, 4, 2)).reshape(depth, -1, C_WIDTH, page)
    w_in_b = w_in.astype(BF16)
    w_out_b = w_out.astype(BF16)
    zero_shift = jnp.zeros((n_p, 1, B_SHIFT_W), F32)
    zero_wkv = jnp.zeros((n_p, B_HEADS, HEAD_DIM, HEAD_DIM), F32)
    seqs_per_tile = A_CHUNK // s_len
    eye_tile = jnp.eye(seqs_per_tile, dtype=F32)

    hp = x_prompt.reshape(n_p * seq_len, d_model)
    hs = x_sample.reshape(n_s * s_len, d_model)
    outs = {k: [] for k in ("ks", "vs", "wkvp", "wkvs", "shp", "shs", "gvs")}
    kv_prompt = None
    for l in range(depth):
        g = norm_g[l].reshape(1, d_model)
        lng_a = a_ln_g[l].reshape(1, A_WIDTH)
        lnb_a = a_ln_b[l].reshape(1, A_WIDTH)
        bs_full = jnp.repeat(a_bs[l].T, HEAD_DIM, axis=1)
        ws_s = jnp.stack([jnp.kron(eye_tile, a_ws[l, h, :s_len, :s_len]) for h in range(A_HEADS)])
        bs_s = jnp.tile(bs_full[:s_len], (seqs_per_tile, 1))
        prm = dict(mu=b_mu[l].reshape(1, -1), w0=b_w0[l].reshape(1, -1), w2=b_w2[l],
                   a0=b_a0[l].reshape(1, -1), a2=b_a2[l], kk=b_kk[l].reshape(1, -1),
                   ka=b_ka[l].reshape(1, -1), rk=b_rk[l].reshape(1, -1),
                   lnx_g=b_lnx_g[l].reshape(1, -1), lnx_b=b_lnx_b[l].reshape(1, -1))
        qg = _tile_heads(c_qn_g[l], C_HEADS)
        kg = _tile_heads(c_kn_g[l], C_HEADS)

        pa, pb, pc = _in_proj(hp, g, w_in_b, l, min(PROJ_ROWS, hp.shape[0]))
        ya, _ = _gmlp(pa, a_ws[l], bs_full, lng_a, lnb_a, gsum_a, GMLP_CHUNKS * A_CHUNK,
                      GMLP_CHUNKS)
        yb, wkvp, shp = _rwkv(pb, zero_shift, zero_wkv, prm, gsum_b, n_p,
                              math.gcd(RWKV_PROMPT_SEQS, n_p), min(RWKV_ROWS, seq_len), RWKV_CHUNK)
        kt, vt, yc = _moba_fused(pc, cos_p, sin_p, qg, kg, gsum_b, n_p, seq_len, kv_prompt)
        kv_prompt = (kt, vt)
        hp = _out_proj(hp, ya, yb, yc, w_out_b, l, min(PROJ_ROWS, hp.shape[0]))
        outs["wkvp"].append(wkvp)
        outs["shp"].append(shp.reshape(n_p, B_SHIFT_W))

        pa, pb, pc = _in_proj(hs, g, w_in_b, l, min(PROJ_ROWS_SAMPLE, hs.shape[0]))
        ya, gv = _gmlp(pa, ws_s, bs_s, lng_a, lnb_a, gsum_a, A_CHUNK, 1)
        yb, wkvs, shs = _rwkv(pb, state_shift[l].reshape(n_s, 1, B_SHIFT_W), state_wkv[l],
                              prm, gsum_b, n_s, math.gcd(RWKV_SAMPLE_SEQS, n_s), s_len, s_len)
        qn, kn = _prep_sample(pc, cos_s, sin_s, qg, kg, gsum_b, min(PREP_ROWS, hs.shape[0]))
        yc = _moba_sample(page_table, qn, kn, pc, cache_kt, cache_vt, l, n_s, s_len,
                          math.gcd(MOBA_SAMPLE_SEQS, n_s))
        hs = _out_proj(hs, ya, yb, yc, w_out_b, l, min(PROJ_ROWS_SAMPLE, hs.shape[0]))
        outs["ks"].append(kn.reshape(n_s, s_len, C_HEADS, HEAD_DIM))
        outs["vs"].append(pc[:, 2 * C_WIDTH:3 * C_WIDTH].reshape(n_s, s_len, C_HEADS, HEAD_DIM))
        outs["wkvs"].append(wkvs)
        outs["shs"].append(shs.reshape(n_s, B_SHIFT_W))
        outs["gvs"].append(gv.reshape(n_s, s_len, A_WIDTH))

    st = {k: jnp.stack(v) for k, v in outs.items()}
    k_prompt = jnp.transpose(kv_prompt[0], (0, 1, 4, 2, 3))
    v_prompt = jnp.transpose(kv_prompt[1], (0, 1, 4, 2, 3))
    return (hp.reshape(n_p, seq_len, d_model), hs.reshape(n_s, s_len, d_model),
            k_prompt, v_prompt, st["ks"], st["vs"], st["wkvp"], st["wkvs"],
            st["shp"], st["shs"], st["gvs"])
```
